```python
import jax, jax.numpy as jnp
from jax import lax
import numpy as np

D_MODEL = 1024
BATCH = 32
SEQ = 2048
DEPTH = 2

GRID_W = 64
CTX_LEN = 256
N_MIXERS = 2
N_MOD = 9
D_FF = 2816
FFN_RES_WEIGHT = 0.5
NORM_EPS = 1e-6
NEG_INF = -1e30
D_RNN = D_MODEL
RG_BLOCKS = 4
RG_BLOCK_W = D_RNN // RG_BLOCKS
CONV_W = 4
CONV_LEFT = 2
LRU_C = 8.0
HEAD_DIM = 64
N_HEADS = D_MODEL // HEAD_DIM
N_KV_HEADS = 4
GROUP = N_HEADS // N_KV_HEADS
WINDOW = 128
Q_BLOCK = 128
ROPE_BASE = 10000.0
ROPE_AXIS_DIM = HEAD_DIM // 2
N_A = (DEPTH + 1) // 2
N_B = DEPTH // 2

kernel_name = 'hybrid_rglru_swa_diffusion'


def rms_norm(x, g):
    xf = x.astype(jnp.float32)
    y = xf * lax.rsqrt(jnp.mean(xf * xf, axis=-1, keepdims=True) + NORM_EPS)
    return (y * g.astype(jnp.float32)).astype(x.dtype)


def modulate(h, g_pre, shift, scale):
    return rms_norm(h, g_pre) * (1 + scale) + shift


def modulation(cond, w, b):
    m = jax.nn.silu(cond) @ w + b
    return jnp.moveaxis(m.reshape(cond.shape[0], N_MOD, D_MODEL), 1, 0)[:, :, None, :]


def swiglu(h, w_in, w_out):
    gate, up = jnp.split(h @ w_in, 2, axis=-1)
    return (jax.nn.silu(gate) * up) @ w_out


def half_ffn_update(h, mod, k, g_pre, g_post, w_in, w_out):
    y = swiglu(modulate(h, g_pre, mod[3 * k], mod[3 * k + 1]), w_in, w_out)
    return h + FFN_RES_WEIGHT * mod[3 * k + 2] * rms_norm(y, g_post)


def centred_dwconv(x, w, b):
    L = x.shape[1]
    xp = jnp.pad(x, ((0, 0), (CONV_LEFT, CONV_W - 1 - CONV_LEFT), (0, 0)))
    out = b
    for k in range(CONV_W):
        out = out + xp[:, k:k + L] * w[k]
    return out


def _scan_combine(e1, e2):
    a1, b1 = e1
    a2, b2 = e2
    return a1 * a2, a2 * b1 + b2


def linear_scan(a, b, h0):
    a_cum, b_cum = lax.associative_scan(_scan_combine, (a, b), axis=1)
    return a_cum * h0[:, None] + b_cum


def rglru_coeffs(xr, gate_w, gate_b, lam):
    B_, L = xr.shape[:2]
    xb = xr.reshape(B_, L, RG_BLOCKS, RG_BLOCK_W)
    g = jnp.einsum('blnj,gnjk->gblnk', xb, gate_w).reshape(2, B_, L, D_RNN) + gate_b[:, None, None]
    g = jax.nn.sigmoid(g.astype(jnp.float32))
    r, i = g[0], g[1]
    log_a = -LRU_C * r * jax.nn.softplus(-lam.astype(jnp.float32))
    a = jnp.exp(log_a)
    b = jnp.sqrt(1.0 - jnp.exp(2.0 * log_a)) * (i * xr.astype(jnp.float32))
    return a, b


def rglru_mixer(hx, hc, w_in, conv_w, conv_b, gate_w, gate_b, lam, w_out, ctx_out):
    def branches(h):
        gate, xr = jnp.split(h @ w_in, 2, axis=-1)
        return jax.nn.gelu(gate), centred_dwconv(xr, conv_w, conv_b)
    gx, rx = branches(hx)
    gc, rc = branches(hc)
    h0 = jnp.zeros((hx.shape[0], D_RNN), jnp.float32)
    a, b = rglru_coeffs(rc, gate_w[0], gate_b[0], lam[0])
    sc_f = linear_scan(a, b, h0)
    a, b = rglru_coeffs(rx, gate_w[0], gate_b[0], lam[0])
    sx_f = linear_scan(a, b, sc_f[:, -1])
    a, b = rglru_coeffs(rc[:, ::-1], gate_w[1], gate_b[1], lam[1])
    sc_b = linear_scan(a, b, h0)
    a, b = rglru_coeffs(rx[:, ::-1], gate_w[1], gate_b[1], lam[1])
    sx_b = linear_scan(a, b, sc_b[:, -1])[:, ::-1]
    yx = (gx * (sx_f + sx_b).astype(hx.dtype)) @ w_out
    yc = (gc * (sc_f + sc_b[:, ::-1]).astype(hc.dtype)) @ w_out if ctx_out else None
    return yx, yc


def axial_rope_angles(L):
    rows = L // GRID_W
    row = jnp.repeat(jnp.arange(rows, dtype=jnp.float32), GRID_W)
    col = jnp.tile(jnp.arange(GRID_W, dtype=jnp.float32), rows)
    inv = 1.0 / (ROPE_BASE ** (jnp.arange(0, ROPE_AXIS_DIM, 2, dtype=jnp.float32) / ROPE_AXIS_DIM))
    return row[:, None] * inv, col[:, None] * inv


def rope_half(x, ang):
    x1, x2 = jnp.split(x, 2, axis=-1)
    cos = jnp.cos(ang)[:, None]
    sin = jnp.sin(ang)[:, None]
    return jnp.concatenate([x1 * cos - x2 * sin, x2 * cos + x1 * sin], axis=-1)


def apply_axial_rope(x, ang_row, ang_col):
    xf = x.astype(jnp.float32)
    x_row, x_col = jnp.split(xf, 2, axis=-1)
    return jnp.concatenate([rope_half(x_row, ang_row), rope_half(x_col, ang_col)], axis=-1).astype(x.dtype)


def softmax_with_sink(scores, sink):
    sink_col = jnp.broadcast_to(sink[None, :, :, None, None], scores.shape[:-1] + (1,))
    p = jax.nn.softmax(jnp.concatenate([scores, sink_col], axis=-1), axis=-1)
    return p[..., :-1]


def window_attention_mixer(hx, hc, w_qkv, w_o, sink, ctx_out):
    B_, L, _ = hx.shape
    C = hc.shape[1]

    def project(h):
        q, k, v = jnp.split(h @ w_qkv, [N_HEADS * HEAD_DIM, (N_HEADS + N_KV_HEADS) * HEAD_DIM], axis=-1)
        n = h.shape[1]
        return (q.reshape(B_, n, N_HEADS, HEAD_DIM), k.reshape(B_, n, N_KV_HEADS, HEAD_DIM),
                v.reshape(B_, n, N_KV_HEADS, HEAD_DIM))

    qx, kx, vx = project(hx)
    qc, kc, vc = project(hc)
    ang_row, ang_col = axial_rope_angles(L)
    qx = apply_axial_rope(qx, ang_row, ang_col).reshape(B_, L, N_KV_HEADS, GROUP, HEAD_DIM)
    kx = apply_axial_rope(kx, ang_row, ang_col)
    qc = qc.reshape(B_, C, N_KV_HEADS, GROUP, HEAD_DIM)
    scale = HEAD_DIM ** -0.5
    sink_kg = sink.reshape(N_KV_HEADS, GROUP).astype(jnp.float32)

    span = Q_BLOCK + 2 * WINDOW
    kp = jnp.pad(kx, ((0, 0), (WINDOW, WINDOW), (0, 0), (0, 0)))
    vp = jnp.pad(vx, ((0, 0), (WINDOW, WINDOW), (0, 0), (0, 0)))
    rel = jnp.arange(span)[None, :] - WINDOW - jnp.arange(Q_BLOCK)[:, None]
    band = jnp.abs(rel) <= WINDOW

    def block(n):
        start = n * Q_BLOCK
        qb = lax.dynamic_slice_in_dim(qx, start, Q_BLOCK, axis=1)
        kb = lax.dynamic_slice_in_dim(kp, start, span, axis=1)
        vb = lax.dynamic_slice_in_dim(vp, start, span, axis=1)
        key_pos = start - WINDOW + jnp.arange(span)
        valid = band & ((key_pos >= 0) & (key_pos < L))[None, :]
        s_lat = jnp.einsum('bqkgd,bskd->bkgqs', qb, kb).astype(jnp.float32) * scale
        s_lat = jnp.where(valid, s_lat, NEG_INF)
        s_ctx = jnp.einsum('bqkgd,bckd->bkgqc', qb, kc).astype(jnp.float32) * scale
        p = softmax_with_sink(jnp.concatenate([s_lat, s_ctx], axis=-1), sink_kg).astype(vx.dtype)
        return (jnp.einsum('bkgqs,bskd->bqkgd', p[..., :span], vb)
                + jnp.einsum('bkgqc,bckd->bqkgd', p[..., span:], vc))

    out = lax.map(block, jnp.arange(L // Q_BLOCK))
    yx = jnp.moveaxis(out, 0, 1).reshape(B_, L, N_HEADS * HEAD_DIM) @ w_o
    if not ctx_out:
        return yx, None
    s_c = jnp.einsum('bqkgd,bckd->bkgqc', qc, kc).astype(jnp.float32) * scale
    p_c = softmax_with_sink(s_c, sink_kg).astype(vc.dtype)
    yc = jnp.einsum('bkgqc,bckd->bqkgd', p_c, vc).reshape(B_, C, N_HEADS * HEAD_DIM) @ w_o
    return yx, yc


def setup_inputs(seed: int = 0) -> dict:
    key = jax.random.key(seed)
    ks = jax.random.split(key, 19)

    def normal(k, shape):
        return jax.random.normal(k, shape, jnp.float32)

    def dense(k, shape, fan_in):
        return normal(k, shape) * fan_in ** -0.5

    a0 = jax.random.uniform(ks[14], (N_A, 2, D_RNN), jnp.float32, minval=0.9, maxval=0.999)
    return {
        'x': normal(ks[0], (BATCH, SEQ, D_MODEL)),
        'c': normal(ks[1], (BATCH, D_MODEL)),
        'ctx': normal(ks[2], (BATCH, CTX_LEN, D_MODEL)),
        'c_ctx': normal(ks[3], (D_MODEL,)),
        'w_mod': 0.5 * dense(ks[4], (DEPTH, D_MODEL, N_MOD * D_MODEL), D_MODEL),
        'b_mod': 0.01 * normal(ks[5], (DEPTH, N_MOD * D_MODEL)),
        'norm_g': 1.0 + 0.02 * normal(ks[6], (DEPTH, 6, D_MODEL)),
        'ffn_w_in': dense(ks[7], (DEPTH, 2, D_MODEL, 2 * D_FF), D_MODEL),
        'ffn_w_out': dense(ks[8], (DEPTH, 2, D_FF, D_MODEL), D_FF),
        'rg_w_in': dense(ks[9], (N_A, D_MODEL, 2 * D_RNN), D_MODEL),
        'rg_conv_w': dense(ks[10], (N_A, CONV_W, D_RNN), CONV_W),
        'rg_conv_b': 0.01 * normal(ks[11], (N_A, D_RNN)),
        'rg_gate_w': dense(ks[12], (N_A, 2, 2, RG_BLOCKS, RG_BLOCK_W, RG_BLOCK_W), RG_BLOCK_W),
        'rg_gate_b': 0.01 * normal(ks[13], (N_A, 2, 2, D_RNN)),
        'rg_lambda': jnp.log(a0) - jnp.log1p(-a0),
        'rg_w_out': dense(ks[15], (N_A, D_RNN, D_MODEL), D_RNN),
        'attn_w_qkv': dense(ks[16], (N_B, D_MODEL, (N_HEADS + 2 * N_KV_HEADS) * HEAD_DIM), D_MODEL),
        'attn_w_o': dense(ks[17], (N_B, N_HEADS * HEAD_DIM, D_MODEL), N_HEADS * HEAD_DIM),
        'attn_sink': normal(ks[18], (N_B, N_HEADS)),
    }


def reference(x, c, ctx, c_ctx, w_mod, b_mod, norm_g, ffn_w_in, ffn_w_out,
              rg_w_in, rg_conv_w, rg_conv_b, rg_gate_w, rg_gate_b, rg_lambda, rg_w_out,
              attn_w_qkv, attn_w_o, attn_sink):
    xc = ctx
    for i in range(DEPTH):
        last = i == DEPTH - 1
        mx = modulation(c, w_mod[i], b_mod[i])
        mc = modulation(c_ctx[None], w_mod[i], b_mod[i])
        g = norm_g[i]
        x = half_ffn_update(x, mx, 0, g[0], g[3], ffn_w_in[i, 0], ffn_w_out[i, 0])
        xc = half_ffn_update(xc, mc, 0, g[0], g[3], ffn_w_in[i, 0], ffn_w_out[i, 0])
        hx = modulate(x, g[1], mx[3], mx[4])
        hc = modulate(xc, g[1], mc[3], mc[4])
        j = i // N_MIXERS
        if i % N_MIXERS == 0:
            yx, yc = rglru_mixer(hx, hc, rg_w_in[j], rg_conv_w[j], rg_conv_b[j], rg_gate_w[j],
                                 rg_gate_b[j], rg_lambda[j], rg_w_out[j], not last)
        else:
            yx, yc = window_attention_mixer(hx, hc, attn_w_qkv[j], attn_w_o[j], attn_sink[j], not last)
        x = x + mx[5] * rms_norm(yx, g[4])
        x = half_ffn_update(x, mx, 2, g[2], g[5], ffn_w_in[i, 1], ffn_w_out[i, 1])
        if not last:
            xc = xc + mc[5] * rms_norm(yc, g[4])
            xc = half_ffn_update(xc, mc, 2, g[2], g[5], ffn_w_in[i, 1], ffn_w_out[i, 1])
    return x
```

```python
import functools

import jax
import jax.numpy as jnp
from jax import lax
from jax.experimental import pallas as pl
from jax.experimental.pallas import tpu as pltpu

f32 = jnp.float32
bf16 = jnp.bfloat16

N_MOD = 9
FFN_RES_WEIGHT = 0.5
NORM_EPS = 1e-6
NEG_INF = -1e30
RG_BLOCKS = 4
CONV_W = 4
LRU_C = 8.0
HEAD_DIM = 64
N_KV_HEADS = 4
WINDOW = 128
GRID_W = 64
ROPE_BASE = 10000.0
ROPE_AXIS_DIM = HEAD_DIM // 2

LANES = 128
SUBLANES = 8
VMEM_LIMIT = 56 * 1024 * 1024
FFN_CHUNK = 256
FFN_ROWS = 512
MIX_ROWS = 256
Q_ROWS = 128


def _cparams(n_axes):
    return pltpu.CompilerParams(dimension_semantics=("arbitrary",) * n_axes,
                                vmem_limit_bytes=VMEM_LIMIT)


def _resident(shape):
    return pl.BlockSpec(shape, lambda *_: (0,) * len(shape), pipeline_mode=pl.Buffered(1))


def _rms(x, g):
    ms = jnp.mean(x * x, axis=-1, keepdims=True)
    return x * lax.rsqrt(ms + NORM_EPS) * g


def _modulate(x, g, shift, scale):
    return _rms(x, g) * (1.0 + scale) + shift


def _mod_kernel(cond_ref, w_ref, b_ref, o_ref):
    c = cond_ref[...]
    s = (c * jax.nn.sigmoid(c)).astype(bf16)
    o_ref[...] = jnp.dot(s, w_ref[...].astype(bf16), preferred_element_type=f32) + b_ref[...]


def _modulation(cond, w_mod, b_mod):
    depth, d, n = w_mod.shape
    r = cond.shape[0]
    tn = 1536
    out = pl.pallas_call(
        _mod_kernel,
        grid=(depth, n // tn),
        in_specs=[pl.BlockSpec((r, d), lambda i, j: (0, 0)),
                  pl.BlockSpec((None, d, tn), lambda i, j: (i, 0, j)),
                  pl.BlockSpec((None, 1, tn), lambda i, j: (i, 0, j))],
        out_specs=pl.BlockSpec((None, r, tn), lambda i, j: (i, 0, j)),
        out_shape=jax.ShapeDtypeStruct((depth, r, n), f32),
        compiler_params=_cparams(2),
        name="modulation",
    )(cond, w_mod, b_mod.reshape(depth, 1, n))
    return out.reshape(depth, r, N_MOD, d)


def _ffn_kernel(x_ref, mod_ref, g_ref, win_ref, wout_ref, o_ref, *, k):
    x = x_ref[...]
    n_chunks, _, two_tf = win_ref.shape
    tf = two_tf // 2
    h = _modulate(x, g_ref[k:k + 1], mod_ref[3 * k:3 * k + 1], mod_ref[3 * k + 1:3 * k + 2])
    hb = h.astype(bf16)

    def body(j, acc):
        gu = jnp.dot(hb, win_ref[j], preferred_element_type=f32)
        gate = gu[:, :tf]
        a = (gate * jax.nn.sigmoid(gate) * gu[:, tf:]).astype(bf16)
        return acc + jnp.dot(a, wout_ref[j], preferred_element_type=f32)

    y = lax.fori_loop(0, n_chunks, body, jnp.zeros(x.shape, f32))
    o_ref[...] = x + FFN_RES_WEIGHT * mod_ref[3 * k + 2:3 * k + 3] * _rms(y, g_ref[3 + k:4 + k])


def _ffn_weights(w_in, w_out):
    d, two_f = w_in.shape
    f = two_f // 2
    n = f // FFN_CHUNK
    wi = w_in.astype(bf16).reshape(d, 2, n, FFN_CHUNK)
    wi = jnp.transpose(wi, (2, 0, 1, 3)).reshape(n, d, 2 * FFN_CHUNK)
    return wi, w_out.astype(bf16).reshape(n, FFN_CHUNK, d)


def _ffn(x, mod, g, weights, k):
    win, wout = weights
    bn, ln, d = x.shape
    tm = min(FFN_ROWS, ln)
    return pl.pallas_call(
        functools.partial(_ffn_kernel, k=k),
        grid=(bn, ln // tm),
        in_specs=[pl.BlockSpec((None, tm, d), lambda b, t: (b, t, 0)),
                  pl.BlockSpec((None, N_MOD, d), lambda b, t: (b, 0, 0)),
                  pl.BlockSpec(g.shape, lambda b, t: (0, 0)),
                  _resident(win.shape),
                  _resident(wout.shape)],
        out_specs=pl.BlockSpec((None, tm, d), lambda b, t: (b, t, 0)),
        out_shape=jax.ShapeDtypeStruct(x.shape, f32),
        compiler_params=_cparams(2),
        name=f"ffn{k}",
    )(x, mod, g, win, wout)


def _group_scan(a, b, h0, reverse):
    t, w = a.shape
    n_groups = t // SUBLANES
    a3 = a.reshape(n_groups, SUBLANES, w)
    b3 = b.reshape(n_groups, SUBLANES, w)
    row = lax.broadcasted_iota(jnp.int32, a3.shape, 1)
    for k in (1, 2, 4):
        shift = SUBLANES - k if reverse else k
        m = (row < SUBLANES - k) if reverse else (row >= k)
        a_sh = pltpu.roll(a3, shift, axis=1)
        b_sh = pltpu.roll(b3, shift, axis=1)
        b3 = jnp.where(m, a3 * b_sh + b3, b3)
        a3 = jnp.where(m, a3 * a_sh, a3)
    hs = [None] * n_groups
    h = h0
    for gi in (range(n_groups - 1, -1, -1) if reverse else range(n_groups)):
        hg = a3[gi] * h + b3[gi]
        hs[gi] = hg
        h = hg[0:1] if reverse else hg[SUBLANES - 1:SUBLANES]
    return jnp.concatenate(hs, axis=0)


def _rg_kernel(x_ref, c_ref, mx_ref, mc_ref, g_ref, win_ref, cw_ref, cb_ref, gw_ref, gb_ref, lam_ref,
               wout_ref, xo_ref, co_ref, xr_s, rx_s, gx_s, sf_s, h_s, *, nx):
    tt, d = x_ref.shape
    bw = d // RG_BLOCKS
    p = pl.program_id(1)
    s = pl.program_id(2)
    is_ctx = s == 0
    mod = jnp.where(is_ctx, mc_ref[...], mx_ref[...])
    pad = SUBLANES

    @pl.when(p == 0)
    def _project():
        @pl.when(is_ctx)
        def _():
            xr_s[0:pad, :] = jnp.zeros((pad, d), f32)
            xr_s[pad + (nx + 1) * tt:2 * pad + (nx + 1) * tt, :] = jnp.zeros((pad, d), f32)
        xt = jnp.where(is_ctx, c_ref[...], x_ref[...])
        h = _modulate(xt, g_ref[1:2], mod[3:4], mod[4:5])
        gr = jnp.dot(h.astype(bf16), win_ref[...], preferred_element_type=f32)
        base = pl.multiple_of(s * tt, tt)
        gx_s[pl.ds(base, tt), :] = jax.nn.gelu(gr[:, :d]).astype(bf16)
        xr_s[pl.ds(base + pad, tt), :] = gr[:, d:]

    def coeffs(rx, direction, blk):
        ri = jnp.dot(rx.astype(bf16), gw_ref[direction, blk], preferred_element_type=f32)
        cols = slice(blk * bw, (blk + 1) * bw)
        r = jax.nn.sigmoid(ri[:, :bw] + gb_ref[2 * direction:2 * direction + 1, cols])
        i = jax.nn.sigmoid(ri[:, bw:] + gb_ref[2 * direction + 1:2 * direction + 2, cols])
        lam = lam_ref[direction:direction + 1, cols]
        a = jnp.exp(r * (-LRU_C * jax.nn.softplus(-lam)))
        return a, jnp.sqrt(1.0 - a * a) * (i * rx)

    @pl.when(p == 1)
    def _forward():
        base = pl.multiple_of(s * tt, tt)
        has_prev = s >= 2
        has_next = jnp.logical_and(s >= 1, s < nx)
        n_ext = tt + 2 * pad
        for blk in range(RG_BLOCKS):
            cols = slice(blk * bw, (blk + 1) * bw)
            prev = jnp.where(has_prev, xr_s[pl.ds(base, pad), cols], 0.0)
            cur = xr_s[pl.ds(base + pad, tt), cols]
            nxt = jnp.where(has_next, xr_s[pl.ds(base + pad + tt, pad), cols], 0.0)
            ext = jnp.concatenate([prev, cur, nxt], axis=0)
            rx = (cb_ref[0:1, cols]
                  + cw_ref[0:1, cols] * pltpu.roll(ext, 2, axis=0)[pad:pad + tt]
                  + cw_ref[1:2, cols] * pltpu.roll(ext, 1, axis=0)[pad:pad + tt]
                  + cw_ref[2:3, cols] * cur
                  + cw_ref[3:4, cols] * pltpu.roll(ext, n_ext - 1, axis=0)[pad:pad + tt])
            rx_s[pl.ds(base, tt), cols] = rx
            a, b = coeffs(rx, 0, blk)
            h0 = jnp.where(is_ctx, 0.0, h_s[0:1, cols])
            hf = _group_scan(a, b, h0, reverse=False)
            sf_s[pl.ds(base, tt), cols] = hf.astype(bf16)
            h_s[0:1, cols] = hf[tt - 1:tt]

    @pl.when(p == 2)
    def _backward():
        tile = jnp.where(is_ctx, 0, nx + 1 - s)
        base = pl.multiple_of(tile * tt, tt)
        zs = []
        for blk in range(RG_BLOCKS):
            cols = slice(blk * bw, (blk + 1) * bw)
            rx = rx_s[pl.ds(base, tt), cols]
            a, b = coeffs(rx, 1, blk)
            h0 = jnp.where(is_ctx, 0.0, h_s[1:2, cols])
            hb = _group_scan(a, b, h0, reverse=True)
            h_s[1:2, cols] = hb[0:1]
            tot = sf_s[pl.ds(base, tt), cols].astype(f32) + hb
            zs.append((gx_s[pl.ds(base, tt), cols].astype(f32) * tot).astype(bf16))
        y = jnp.dot(jnp.concatenate(zs, axis=1), wout_ref[...], preferred_element_type=f32)
        upd = mod[5:6] * _rms(y, g_ref[4:5])

        @pl.when(is_ctx)
        def _():
            co_ref[...] = c_ref[...] + upd

        @pl.when(jnp.logical_not(is_ctx))
        def _():
            xo_ref[...] = x_ref[...] + upd


def _rglru(x, xc, mx, mc, g, w_in, conv_w, conv_b, gate_w, gate_b, lam, w_out):
    b, l, d = x.shape
    c = xc.shape[1]
    tt = MIX_ROWS
    assert c == tt and l % tt == 0, "context must be exactly one mixer tile"
    nx = l // tt
    bw = d // RG_BLOCKS
    gw = jnp.concatenate([gate_w[:, 0], gate_w[:, 1]], axis=-1).astype(bf16)
    gb = gate_b.reshape(4, d)
    rows = (nx + 1) * tt

    def x_tile(p, s):
        return jnp.where(p == 0, jnp.maximum(s - 1, 0), jnp.where(p == 1, nx - 1, nx - jnp.maximum(s, 1)))

    def xo_tile(p, s):
        return jnp.where(p == 2, nx - jnp.maximum(s, 1), nx - 1)

    return pl.pallas_call(
        functools.partial(_rg_kernel, nx=nx),
        grid=(b, 3, nx + 1),
        in_specs=[pl.BlockSpec((None, tt, d), lambda i, p, s: (i, x_tile(p, s), 0)),
                  pl.BlockSpec((None, tt, d), lambda i, p, s: (i, 0, 0)),
                  pl.BlockSpec((None, N_MOD, d), lambda i, p, s: (i, 0, 0)),
                  pl.BlockSpec((None, N_MOD, d), lambda i, p, s: (0, 0, 0)),
                  pl.BlockSpec(g.shape, lambda i, p, s: (0, 0)),
                  _resident((d, 2 * d)),
                  pl.BlockSpec(conv_w.shape, lambda i, p, s: (0, 0)),
                  pl.BlockSpec((1, d), lambda i, p, s: (0, 0)),
                  _resident(gw.shape),
                  pl.BlockSpec(gb.shape, lambda i, p, s: (0, 0)),
                  pl.BlockSpec(lam.shape, lambda i, p, s: (0, 0)),
                  _resident((d, d))],
        out_specs=[pl.BlockSpec((None, tt, d), lambda i, p, s: (i, xo_tile(p, s), 0)),
                   pl.BlockSpec((None, tt, d), lambda i, p, s: (i, 0, 0))],
        out_shape=[jax.ShapeDtypeStruct(x.shape, f32), jax.ShapeDtypeStruct(xc.shape, f32)],
        scratch_shapes=[pltpu.VMEM((rows + 2 * SUBLANES, d), f32),
                        pltpu.VMEM((rows, d), f32),
                        pltpu.VMEM((rows, d), bf16),
                        pltpu.VMEM((rows, d), bf16),
                        pltpu.VMEM((SUBLANES, d), f32)],
        compiler_params=_cparams(3),
        name="rglru",
    )(x, xc, mx, mc, g, w_in.astype(bf16), conv_w, conv_b.reshape(1, d), gw, gb, lam, w_out.astype(bf16))


def _rope_tables(l):
    rows = l // GRID_W
    row = jnp.repeat(jnp.arange(rows, dtype=f32), GRID_W)
    col = jnp.tile(jnp.arange(GRID_W, dtype=f32), rows)
    inv = 1.0 / (ROPE_BASE ** (jnp.arange(0, ROPE_AXIS_DIM, 2, dtype=f32) / ROPE_AXIS_DIM))
    ar, ac = row[:, None] * inv, col[:, None] * inv
    cos = jnp.concatenate([jnp.cos(ar), jnp.cos(ar), jnp.cos(ac), jnp.cos(ac)], axis=-1)
    sin = jnp.concatenate([-jnp.sin(ar), jnp.sin(ar), -jnp.sin(ac), jnp.sin(ac)], axis=-1)
    reps = LANES // HEAD_DIM
    return jnp.tile(cos, (1, reps)), jnp.tile(sin, (1, reps))


def _qkv_kernel(x_ref, mod_ref, g_ref, w_ref, cos_ref, sin_ref, q_ref, k_ref, v_ref, *, dq, dkv):
    h = _modulate(x_ref[...], g_ref[1:2], mod_ref[3:4], mod_ref[4:5])
    qkv = jnp.dot(h.astype(bf16), w_ref[...], preferred_element_type=f32)
    cos = cos_ref[...]
    sin = sin_ref[...]
    lane = lax.broadcasted_iota(jnp.int32, cos.shape, 1)
    first = (lane % (ROPE_AXIS_DIM)) < (ROPE_AXIS_DIM // 2)
    half = ROPE_AXIS_DIM // 2

    def rope(z):
        partner = jnp.where(first, pltpu.roll(z, LANES - half, axis=1), pltpu.roll(z, half, axis=1))
        return z * cos + partner * sin

    scale = HEAD_DIM ** -0.5
    for j in range(dq // LANES):
        cols = slice(j * LANES, (j + 1) * LANES)
        q_ref[:, cols] = (rope(qkv[:, cols]) * scale).astype(bf16)
    for j in range(dkv // LANES):
        k_ref[:, j * LANES:(j + 1) * LANES] = rope(qkv[:, dq + j * LANES:dq + (j + 1) * LANES]).astype(bf16)
    v_ref[...] = qkv[:, dq + dkv:].astype(bf16)


def _kv_kernel(x_ref, mod_ref, g_ref, w_ref, k_ref, v_ref, *, dkv):
    h = _modulate(x_ref[...], g_ref[1:2], mod_ref[3:4], mod_ref[4:5])
    kv = jnp.dot(h.astype(bf16), w_ref[...], preferred_element_type=f32)
    k_ref[...] = kv[:, :dkv].astype(bf16)
    v_ref[...] = kv[:, dkv:].astype(bf16)


def _attn_kernel(sink_ref, x_ref, q_ref, k_ref, v_ref, kc_ref, vc_ref, mod_ref, g_ref, wo_ref, o_ref):
    tq, d = x_ref.shape
    l = k_ref.shape[0]
    span = tq + 2 * WINDOW
    group = d // HEAD_DIM // N_KV_HEADS
    start = pl.program_id(1) * tq
    ws = pl.multiple_of(jnp.clip(start - WINDOW, 0, l - span), LANES)
    kw = k_ref[pl.ds(ws, span), :]
    vw = v_ref[pl.ds(ws, span), :]
    kc = kc_ref[...]
    vc = vc_ref[...]
    q = q_ref[...]
    qi = lax.broadcasted_iota(jnp.int32, (group * tq, span), 0) % tq
    kj = lax.broadcasted_iota(jnp.int32, (group * tq, span), 1)
    valid = jnp.abs(kj - qi + (ws - start)) <= WINDOW
    nt = (((1,), (1,)), ((), ()))
    outs = []
    for kh in range(N_KV_HEADS):
        heads = range(kh * group, (kh + 1) * group)
        qs = jnp.concatenate([q[:, h * HEAD_DIM:(h + 1) * HEAD_DIM] for h in heads], axis=0)
        cols = slice(kh * HEAD_DIM, (kh + 1) * HEAD_DIM)
        s_lat = jnp.where(valid, lax.dot_general(qs, kw[:, cols], nt, preferred_element_type=f32), NEG_INF)
        s_ctx = lax.dot_general(qs, kc[:, cols], nt, preferred_element_type=f32)
        sink = jnp.concatenate([jnp.full((tq, 1), sink_ref[h], f32) for h in heads], axis=0)
        m = jnp.maximum(jnp.maximum(jnp.max(s_lat, axis=-1, keepdims=True),
                                    jnp.max(s_ctx, axis=-1, keepdims=True)), sink)
        p_lat = jnp.exp(s_lat - m)
        p_ctx = jnp.exp(s_ctx - m)
        den = (jnp.sum(p_lat, axis=-1, keepdims=True) + jnp.sum(p_ctx, axis=-1, keepdims=True)
               + jnp.exp(sink - m))
        o = (jnp.dot(p_lat.astype(bf16), vw[:, cols], preferred_element_type=f32)
             + jnp.dot(p_ctx.astype(bf16), vc[:, cols], preferred_element_type=f32)) / den
        outs += [o[i * tq:(i + 1) * tq] for i in range(group)]
    att = jnp.concatenate(outs, axis=1).astype(bf16)
    y = jnp.dot(att, wo_ref[...], preferred_element_type=f32)
    o_ref[...] = x_ref[...] + mod_ref[5:6] * _rms(y, g_ref[4:5])


def _attention(x, xc, mx, mc, g, w_qkv, w_o, sink):
    b, l, d = x.shape
    c = xc.shape[1]
    dq = w_o.shape[0]
    dkv = (w_qkv.shape[1] - dq) // 2
    tt = min(MIX_ROWS, l)
    cos, sin = _rope_tables(l)
    wb = w_qkv.astype(bf16)
    q, k, v = pl.pallas_call(
        functools.partial(_qkv_kernel, dq=dq, dkv=dkv),
        grid=(b, l // tt),
        in_specs=[pl.BlockSpec((None, tt, d), lambda i, t: (i, t, 0)),
                  pl.BlockSpec((None, N_MOD, d), lambda i, t: (i, 0, 0)),
                  pl.BlockSpec(g.shape, lambda i, t: (0, 0)),
                  _resident(wb.shape),
                  pl.BlockSpec((tt, LANES), lambda i, t: (t, 0)),
                  pl.BlockSpec((tt, LANES), lambda i, t: (t, 0))],
        out_specs=[pl.BlockSpec((None, tt, dq), lambda i, t: (i, t, 0)),
                   pl.BlockSpec((None, tt, dkv), lambda i, t: (i, t, 0)),
                   pl.BlockSpec((None, tt, dkv), lambda i, t: (i, t, 0))],
        out_shape=[jax.ShapeDtypeStruct((b, l, dq), bf16),
                   jax.ShapeDtypeStruct((b, l, dkv), bf16),
                   jax.ShapeDtypeStruct((b, l, dkv), bf16)],
        compiler_params=_cparams(2),
        name="attn_qkv",
    )(x, mx, g, wb, cos, sin)
    kc, vc = pl.pallas_call(
        functools.partial(_kv_kernel, dkv=dkv),
        grid=(b,),
        in_specs=[pl.BlockSpec((None, c, d), lambda i: (i, 0, 0)),
                  pl.BlockSpec((None, N_MOD, d), lambda i: (0, 0, 0)),
                  pl.BlockSpec(g.shape, lambda i: (0, 0)),
                  _resident((d, 2 * dkv))],
        out_specs=[pl.BlockSpec((None, c, dkv), lambda i: (i, 0, 0)),
                   pl.BlockSpec((None, c, dkv), lambda i: (i, 0, 0))],
        out_shape=[jax.ShapeDtypeStruct((b, c, dkv), bf16), jax.ShapeDtypeStruct((b, c, dkv), bf16)],
        compiler_params=_cparams(1),
        name="attn_ctx_kv",
    )(xc, mc, g, wb[:, dq:])
    tq = Q_ROWS
    assert l % tq == 0 and l >= tq + 2 * WINDOW
    return pl.pallas_call(
        _attn_kernel,
        grid=(b, l // tq),
        in_specs=[pl.BlockSpec(memory_space=pltpu.SMEM),
                  pl.BlockSpec((None, tq, d), lambda i, n: (i, n, 0)),
                  pl.BlockSpec((None, tq, dq), lambda i, n: (i, n, 0)),
                  pl.BlockSpec((None, l, dkv), lambda i, n: (i, 0, 0)),
                  pl.BlockSpec((None, l, dkv), lambda i, n: (i, 0, 0)),
                  pl.BlockSpec((None, c, dkv), lambda i, n: (i, 0, 0)),
                  pl.BlockSpec((None, c, dkv), lambda i, n: (i, 0, 0)),
                  pl.BlockSpec((None, N_MOD, d), lambda i, n: (i, 0, 0)),
                  pl.BlockSpec(g.shape, lambda i, n: (0, 0)),
                  _resident((dq, d))],
        out_specs=pl.BlockSpec((None, tq, d), lambda i, n: (i, n, 0)),
        out_shape=jax.ShapeDtypeStruct(x.shape, f32),
        compiler_params=_cparams(2),
        name="attn_core",
    )(sink, x, q, k, v, kc, vc, mx, g, w_o.astype(bf16))


def kernel(x, c, ctx, c_ctx, w_mod, b_mod, norm_g, ffn_w_in, ffn_w_out, rg_w_in, rg_conv_w, rg_conv_b,
           rg_gate_w, rg_gate_b, rg_lambda, rg_w_out, attn_w_qkv, attn_w_o, attn_sink):
    b, l, d = x.shape
    n_ctx = ctx.shape[1]
    depth = w_mod.shape[0]
    assert depth == 2, "layer 0 = RG-LRU with context output, layer 1 = windowed attention (last)"
    rows = b + 1
    rows_pad = -(-rows // 16) * 16
    cond = jnp.zeros((rows_pad, d), f32).at[:b].set(c).at[b].set(c_ctx)
    mods = _modulation(cond, w_mod, b_mod)
    xc = ctx
    for i in range(depth):
        mx, mc, g = mods[i, :b], mods[i, b:b + 1], norm_g[i]
        flat = lambda a: a.reshape(1, b * n_ctx, d)
        w1 = _ffn_weights(ffn_w_in[i, 0], ffn_w_out[i, 0])
        w2 = _ffn_weights(ffn_w_in[i, 1], ffn_w_out[i, 1])
        x = _ffn(x, mx, g, w1, 0)
        xc = _ffn(flat(xc), mc, g, w1, 0).reshape(b, n_ctx, d)
        if i == 0:
            x, xc = _rglru(x, xc, mx, mc, g, rg_w_in[0], rg_conv_w[0], rg_conv_b[0], rg_gate_w[0],
                           rg_gate_b[0], rg_lambda[0], rg_w_out[0])
            x = _ffn(x, mx, g, w2, 2)
            xc = _ffn(flat(xc), mc, g, w2, 2).reshape(b, n_ctx, d)
        else:
            x = _attention(x, xc, mx, mc, g, attn_w_qkv[0], attn_w_o[0], attn_sink[0])
            x = _ffn(x, mx, g, w2, 2)
    return x
```

```python
import functools

import jax
import jax.numpy as jnp
from jax import lax
from jax.experimental import pallas as pl
from jax.experimental.pallas import tpu as pltpu

f32 = jnp.float32
bf16 = jnp.bfloat16

N_MOD = 9
FFN_RES_WEIGHT = 0.5
NORM_EPS = 1e-6
NEG_INF = -1e30
RG_BLOCKS = 4
CONV_W = 4
LRU_C = 8.0
HEAD_DIM = 64
N_KV_HEADS = 4
WINDOW = 128
GRID_W = 64
ROPE_BASE = 10000.0
ROPE_AXIS_DIM = HEAD_DIM // 2
LOG2E = 1.4426950408889634
Q_SCALE = HEAD_DIM ** -0.5 * LOG2E

LANES = 128
SUBLANES = 8
VMEM_LIMIT = 56 * 1024 * 1024
FFN_CHUNK = 256
FFN_ROWS = 512
MIX_ROWS = 256
Q_ROWS = 128


def _cparams(n_axes):
    return pltpu.CompilerParams(dimension_semantics=("arbitrary",) * n_axes,
                                vmem_limit_bytes=VMEM_LIMIT)


def _resident(shape):
    return pl.BlockSpec(shape, lambda *_: (0,) * len(shape), pipeline_mode=pl.Buffered(1))


def _rms(x, gain):
    ms = jnp.mean(x * x, axis=-1, keepdims=True)
    return x * lax.rsqrt(ms + NORM_EPS) * gain


def _modulate(x, g, shift, scale):
    return _rms(x, g * (1.0 + scale)) + shift


def _mod_kernel(cond_ref, w_ref, b_ref, o_ref):
    c = cond_ref[...]
    s = (c * jax.nn.sigmoid(c)).astype(bf16)
    o_ref[...] = jnp.dot(s, w_ref[...].astype(bf16), preferred_element_type=f32) + b_ref[...]


def _modulation(cond, w_mod, b_mod):
    depth, d, n = w_mod.shape
    r = cond.shape[0]
    tn = 1536
    out = pl.pallas_call(
        _mod_kernel,
        grid=(depth, n // tn),
        in_specs=[pl.BlockSpec((r, d), lambda i, j: (0, 0)),
                  pl.BlockSpec((None, d, tn), lambda i, j: (i, 0, j)),
                  pl.BlockSpec((None, 1, tn), lambda i, j: (i, 0, j))],
        out_specs=pl.BlockSpec((None, r, tn), lambda i, j: (i, 0, j)),
        out_shape=jax.ShapeDtypeStruct((depth, r, n), f32),
        compiler_params=_cparams(2),
        name="modulation",
    )(cond, w_mod, b_mod.reshape(depth, 1, n))
    return out.reshape(depth, r, N_MOD, d)


def _ffn_kernel(x_ref, mod_ref, g_ref, win_ref, wout_ref, o_ref, *, k):
    x = x_ref[...]
    n_chunks, _, two_tf = win_ref.shape
    tf = two_tf // 2
    h = _modulate(x, g_ref[k:k + 1], mod_ref[3 * k:3 * k + 1], mod_ref[3 * k + 1:3 * k + 2])
    hb = h.astype(bf16)

    def hidden(j):
        gu = jnp.dot(hb, win_ref[j], preferred_element_type=f32)
        gate = gu[:, :tf]
        return (gate * jax.nn.sigmoid(gate) * gu[:, tf:]).astype(bf16)

    a_prev = hidden(0)
    y = None
    for j in range(1, n_chunks + 1):
        part = jnp.dot(a_prev, wout_ref[j - 1], preferred_element_type=f32)
        y = part if y is None else y + part
        if j < n_chunks:
            a_prev = hidden(j)
    o_ref[...] = x + _rms(y, FFN_RES_WEIGHT * mod_ref[3 * k + 2:3 * k + 3] * g_ref[3 + k:4 + k])


def _ffn_weights(w_in, w_out):
    d, two_f = w_in.shape
    f = two_f // 2
    n = f // FFN_CHUNK
    wi = w_in.astype(bf16).reshape(d, 2, n, FFN_CHUNK)
    wi = jnp.transpose(wi, (2, 0, 1, 3)).reshape(n, d, 2 * FFN_CHUNK)
    return wi, w_out.astype(bf16).reshape(n, FFN_CHUNK, d)


def _ffn(x, mod, g, weights, k):
    win, wout = weights
    bn, ln, d = x.shape
    tm = min(FFN_ROWS, ln)
    return pl.pallas_call(
        functools.partial(_ffn_kernel, k=k),
        grid=(bn, ln // tm),
        in_specs=[pl.BlockSpec((None, tm, d), lambda b, t: (b, t, 0)),
                  pl.BlockSpec((None, N_MOD, d), lambda b, t: (b, 0, 0)),
                  pl.BlockSpec(g.shape, lambda b, t: (0, 0)),
                  _resident(win.shape),
                  _resident(wout.shape)],
        out_specs=pl.BlockSpec((None, tm, d), lambda b, t: (b, t, 0)),
        out_shape=jax.ShapeDtypeStruct(x.shape, f32),
        compiler_params=_cparams(2),
        name=f"ffn{k}",
    )(x, mod, g, win, wout)


def _group_scan(a, b, h0, reverse):
    t, w = a.shape
    n_groups = t // SUBLANES
    a3 = a.reshape(n_groups, SUBLANES, w)
    b3 = b.reshape(n_groups, SUBLANES, w)
    row = lax.broadcasted_iota(jnp.int32, a3.shape, 1)
    for k in (1, 2, 4):
        shift = SUBLANES - k if reverse else k
        m = (row < SUBLANES - k) if reverse else (row >= k)
        a_sh = pltpu.roll(a3, shift, axis=1)
        b_sh = pltpu.roll(b3, shift, axis=1)
        b3 = jnp.where(m, a3 * b_sh + b3, b3)
        a3 = jnp.where(m, a3 * a_sh, a3)
    hs = [None] * n_groups
    h = h0
    for gi in (range(n_groups - 1, -1, -1) if reverse else range(n_groups)):
        hg = a3[gi] * h + b3[gi]
        hs[gi] = hg
        h = hg[0:1] if reverse else hg[SUBLANES - 1:SUBLANES]
    return jnp.concatenate(hs, axis=0)


def _rg_kernel(x_ref, c_ref, mx_ref, mc_ref, g_ref, win_ref, cw_ref, cb_ref, gw_ref, gb_ref, lam_ref,
               wout_ref, xo_ref, co_ref, xr_s, rx_s, gx_s, sf_s, h_s, *, nx):
    tt, d = x_ref.shape
    bw = d // RG_BLOCKS
    p = pl.program_id(1)
    s = pl.program_id(2)
    is_ctx = s == 0
    mod = jnp.where(is_ctx, mc_ref[...], mx_ref[...])
    pad = SUBLANES

    @pl.when(p == 0)
    def _project():
        @pl.when(is_ctx)
        def _():
            xr_s[0:pad, :] = jnp.zeros((pad, d), f32)
            xr_s[pad + (nx + 1) * tt:2 * pad + (nx + 1) * tt, :] = jnp.zeros((pad, d), f32)
        xt = jnp.where(is_ctx, c_ref[...], x_ref[...])
        h = _modulate(xt, g_ref[1:2], mod[3:4], mod[4:5])
        gr = jnp.dot(h.astype(bf16), win_ref[...], preferred_element_type=f32)
        base = pl.multiple_of(s * tt, tt)
        gx_s[pl.ds(base, tt), :] = jax.nn.gelu(gr[:, :d]).astype(bf16)
        xr_s[pl.ds(base + pad, tt), :] = gr[:, d:]

    def coeffs(rx, direction, blk):
        ri = jnp.dot(rx.astype(bf16), gw_ref[direction, blk], preferred_element_type=f32)
        cols = slice(blk * bw, (blk + 1) * bw)
        r = jax.nn.sigmoid(ri[:, :bw] + gb_ref[2 * direction:2 * direction + 1, cols])
        i = jax.nn.sigmoid(ri[:, bw:] + gb_ref[2 * direction + 1:2 * direction + 2, cols])
        lam = lam_ref[direction:direction + 1, cols]
        a = jnp.exp(r * (-LRU_C * jax.nn.softplus(-lam)))
        return a, jnp.sqrt(1.0 - a * a) * (i * rx)

    @pl.when(p == 1)
    def _forward():
        base = pl.multiple_of(s * tt, tt)
        has_prev = s >= 2
        has_next = jnp.logical_and(s >= 1, s < nx)
        n_ext = tt + 2 * pad
        for blk in range(RG_BLOCKS):
            cols = slice(blk * bw, (blk + 1) * bw)
            prev = jnp.where(has_prev, xr_s[pl.ds(base, pad), cols], 0.0)
            cur = xr_s[pl.ds(base + pad, tt), cols]
            nxt = jnp.where(has_next, xr_s[pl.ds(base + pad + tt, pad), cols], 0.0)
            ext = jnp.concatenate([prev, cur, nxt], axis=0)
            rx = (cb_ref[0:1, cols]
                  + cw_ref[0:1, cols] * pltpu.roll(ext, 2, axis=0)[pad:pad + tt]
                  + cw_ref[1:2, cols] * pltpu.roll(ext, 1, axis=0)[pad:pad + tt]
                  + cw_ref[2:3, cols] * cur
                  + cw_ref[3:4, cols] * pltpu.roll(ext, n_ext - 1, axis=0)[pad:pad + tt])
            rx_s[pl.ds(base, tt), cols] = rx
            a, b = coeffs(rx, 0, blk)
            h0 = jnp.where(is_ctx, 0.0, h_s[0:1, cols])
            hf = _group_scan(a, b, h0, reverse=False)
            sf_s[pl.ds(base, tt), cols] = hf.astype(bf16)
            h_s[0:1, cols] = hf[tt - 1:tt]

    @pl.when(p == 2)
    def _backward():
        tile = jnp.where(is_ctx, 0, nx + 1 - s)
        base = pl.multiple_of(tile * tt, tt)
        zs = []
        for blk in range(RG_BLOCKS):
            cols = slice(blk * bw, (blk + 1) * bw)
            rx = rx_s[pl.ds(base, tt), cols]
            a, b = coeffs(rx, 1, blk)
            h0 = jnp.where(is_ctx, 0.0, h_s[1:2, cols])
            hb = _group_scan(a, b, h0, reverse=True)
            h_s[1:2, cols] = hb[0:1]
            tot = sf_s[pl.ds(base, tt), cols].astype(f32) + hb
            zs.append((gx_s[pl.ds(base, tt), cols].astype(f32) * tot).astype(bf16))
        y = jnp.dot(jnp.concatenate(zs, axis=1), wout_ref[...], preferred_element_type=f32)
        upd = _rms(y, mod[5:6] * g_ref[4:5])

        @pl.when(is_ctx)
        def _():
            co_ref[...] = c_ref[...] + upd

        @pl.when(jnp.logical_not(is_ctx))
        def _():
            xo_ref[...] = x_ref[...] + upd


def _rglru(x, xc, mx, mc, g, w_in, conv_w, conv_b, gate_w, gate_b, lam, w_out):
    b, l, d = x.shape
    c = xc.shape[1]
    tt = MIX_ROWS
    assert c == tt and l % tt == 0, "context must be exactly one mixer tile"
    nx = l // tt
    bw = d // RG_BLOCKS
    gw = jnp.concatenate([gate_w[:, 0], gate_w[:, 1]], axis=-1).astype(bf16)
    gb = gate_b.reshape(4, d)
    rows = (nx + 1) * tt

    def x_tile(p, s):
        return jnp.where(p == 0, jnp.maximum(s - 1, 0), jnp.where(p == 1, nx - 1, nx - jnp.maximum(s, 1)))

    def xo_tile(p, s):
        return jnp.where(p == 2, nx - jnp.maximum(s, 1), nx - 1)

    return pl.pallas_call(
        functools.partial(_rg_kernel, nx=nx),
        grid=(b, 3, nx + 1),
        in_specs=[pl.BlockSpec((None, tt, d), lambda i, p, s: (i, x_tile(p, s), 0)),
                  pl.BlockSpec((None, tt, d), lambda i, p, s: (i, 0, 0)),
                  pl.BlockSpec((None, N_MOD, d), lambda i, p, s: (i, 0, 0)),
                  pl.BlockSpec((None, N_MOD, d), lambda i, p, s: (0, 0, 0)),
                  pl.BlockSpec(g.shape, lambda i, p, s: (0, 0)),
                  _resident((d, 2 * d)),
                  pl.BlockSpec(conv_w.shape, lambda i, p, s: (0, 0)),
                  pl.BlockSpec((1, d), lambda i, p, s: (0, 0)),
                  _resident(gw.shape),
                  pl.BlockSpec(gb.shape, lambda i, p, s: (0, 0)),
                  pl.BlockSpec(lam.shape, lambda i, p, s: (0, 0)),
                  _resident((d, d))],
        out_specs=[pl.BlockSpec((None, tt, d), lambda i, p, s: (i, xo_tile(p, s), 0)),
                   pl.BlockSpec((None, tt, d), lambda i, p, s: (i, 0, 0))],
        out_shape=[jax.ShapeDtypeStruct(x.shape, f32), jax.ShapeDtypeStruct(xc.shape, f32)],
        scratch_shapes=[pltpu.VMEM((rows + 2 * SUBLANES, d), f32),
                        pltpu.VMEM((rows, d), f32),
                        pltpu.VMEM((rows, d), bf16),
                        pltpu.VMEM((rows, d), bf16),
                        pltpu.VMEM((SUBLANES, d), f32)],
        compiler_params=_cparams(3),
        name="rglru",
    )(x, xc, mx, mc, g, w_in.astype(bf16), conv_w, conv_b.reshape(1, d), gw, gb, lam, w_out.astype(bf16))


def _rope_tables(l):
    rows = l // GRID_W
    row = jnp.repeat(jnp.arange(rows, dtype=f32), GRID_W)
    col = jnp.tile(jnp.arange(GRID_W, dtype=f32), rows)
    inv = 1.0 / (ROPE_BASE ** (jnp.arange(0, ROPE_AXIS_DIM, 2, dtype=f32) / ROPE_AXIS_DIM))
    ar, ac = row[:, None] * inv, col[:, None] * inv
    cos = jnp.concatenate([jnp.cos(ar), jnp.cos(ar), jnp.cos(ac), jnp.cos(ac)], axis=-1)
    sin = jnp.concatenate([-jnp.sin(ar), jnp.sin(ar), -jnp.sin(ac), jnp.sin(ac)], axis=-1)
    reps = LANES // HEAD_DIM
    return jnp.tile(cos, (1, reps)), jnp.tile(sin, (1, reps))


def _qkv_kernel(x_ref, mod_ref, g_ref, w_ref, cos_ref, sin_ref, qt_ref, k_ref, vt_ref, *, dq, dkv):
    tt = x_ref.shape[0]
    h = _modulate(x_ref[...], g_ref[1:2], mod_ref[3:4], mod_ref[4:5])
    qkv = jnp.dot(h.astype(bf16), w_ref[...], preferred_element_type=f32)
    cos = cos_ref[...]
    sin = sin_ref[...]
    lane = lax.broadcasted_iota(jnp.int32, cos.shape, 1)
    half = ROPE_AXIS_DIM // 2
    first = (lane % ROPE_AXIS_DIM) < half

    def rope(z):
        partner = jnp.where(first, pltpu.roll(z, LANES - half, axis=1), pltpu.roll(z, half, axis=1))
        return z * cos + partner * sin

    for j in range(dq // LANES):
        cols = slice(j * LANES, (j + 1) * LANES)
        qt_ref[cols, :] = (rope(qkv[:, cols]) * Q_SCALE).T.astype(bf16)
    for j in range(dkv // LANES):
        k_ref[:, j * LANES:(j + 1) * LANES] = rope(qkv[:, dq + j * LANES:dq + (j + 1) * LANES]).astype(bf16)
    vt = qkv[:, dq + dkv:].T.astype(bf16)
    for j in range(tt // LANES):
        vt_ref[j] = vt[:, j * LANES:(j + 1) * LANES]


def _kv_kernel(x_ref, mod_ref, g_ref, w_ref, k_ref, vt_ref, *, dkv):
    h = _modulate(x_ref[...], g_ref[1:2], mod_ref[3:4], mod_ref[4:5])
    kv = jnp.dot(h.astype(bf16), w_ref[...], preferred_element_type=f32)
    k_ref[...] = kv[:, :dkv].astype(bf16)
    vt_ref[...] = kv[:, dkv:].T.astype(bf16)


def _attn_kernel(sink_ref, x_ref, qt_ref, k_ref, vt_ref, kc_ref, vct_ref, mod_ref, g_ref, wo_ref, o_ref):
    tq, d = x_ref.shape
    l = k_ref.shape[0]
    span = tq + 2 * WINDOW
    group = d // HEAD_DIM // N_KV_HEADS
    lanes = group * tq
    start = pl.program_id(1) * tq
    ws = pl.multiple_of(jnp.clip(start - WINDOW, 0, l - span), LANES)
    kw = k_ref[pl.ds(ws, span), :]
    slab = ws // LANES
    vtw = jnp.concatenate([vt_ref[slab + i] for i in range(span // LANES)], axis=1)
    kc = kc_ref[...]
    vct = vct_ref[...]
    qt = qt_ref[...]
    kj = lax.broadcasted_iota(jnp.int32, (span, lanes), 0)
    qi = lax.broadcasted_iota(jnp.int32, (span, lanes), 1) % tq
    valid = jnp.abs(kj - qi + (ws - start)) <= WINDOW

    def with_ones(v):
        row = lax.broadcasted_iota(jnp.int32, (2 * SUBLANES, v.shape[1]), 0)
        return jnp.concatenate([v, (row == 0).astype(bf16)], axis=0)

    pieces = []
    for kh in range(N_KV_HEADS):
        heads = range(kh * group, (kh + 1) * group)
        qg = jnp.concatenate([qt[h * HEAD_DIM:(h + 1) * HEAD_DIM] for h in heads], axis=1)
        pair, odd = divmod(kh, LANES // HEAD_DIM)
        zero = jnp.zeros_like(qg)
        qz = jnp.concatenate([zero, qg] if odd else [qg, zero], axis=0)
        kcols = slice(pair * LANES, (pair + 1) * LANES)
        s_lat = jnp.where(valid, jnp.dot(kw[:, kcols], qz, preferred_element_type=f32), NEG_INF)
        s_ctx = jnp.dot(kc[:, kcols], qz, preferred_element_type=f32)
        sink = jnp.concatenate([jnp.full((1, tq), sink_ref[h] * LOG2E, f32) for h in heads], axis=1)
        m = jnp.maximum(jnp.maximum(jnp.max(s_lat, axis=0, keepdims=True),
                                    jnp.max(s_ctx, axis=0, keepdims=True)), sink)
        p_lat = jnp.exp2(s_lat - m).astype(bf16)
        p_ctx = jnp.exp2(s_ctx - m).astype(bf16)
        vrows = slice(kh * HEAD_DIM, (kh + 1) * HEAD_DIM)
        o = (jnp.dot(with_ones(vtw[vrows]), p_lat, preferred_element_type=f32)
             + jnp.dot(with_ones(vct[vrows]), p_ctx, preferred_element_type=f32))
        den = o[HEAD_DIM:HEAD_DIM + 1] + jnp.exp2(sink - m)
        o = o[:HEAD_DIM] / den
        pieces += [o[:, i * tq:(i + 1) * tq] for i in range(group)]
    att = jnp.concatenate(pieces, axis=0).T.astype(bf16)
    y = jnp.dot(att, wo_ref[...], preferred_element_type=f32)
    o_ref[...] = x_ref[...] + _rms(y, mod_ref[5:6] * g_ref[4:5])


def _attention(x, xc, mx, mc, g, w_qkv, w_o, sink):
    b, l, d = x.shape
    c = xc.shape[1]
    dq = w_o.shape[0]
    dkv = (w_qkv.shape[1] - dq) // 2
    tt = min(MIX_ROWS, l)
    cos, sin = _rope_tables(l)
    wb = w_qkv.astype(bf16)
    qt, k, vt = pl.pallas_call(
        functools.partial(_qkv_kernel, dq=dq, dkv=dkv),
        grid=(b, l // tt),
        in_specs=[pl.BlockSpec((None, tt, d), lambda i, t: (i, t, 0)),
                  pl.BlockSpec((None, N_MOD, d), lambda i, t: (i, 0, 0)),
                  pl.BlockSpec(g.shape, lambda i, t: (0, 0)),
                  _resident(wb.shape),
                  pl.BlockSpec((tt, LANES), lambda i, t: (t, 0)),
                  pl.BlockSpec((tt, LANES), lambda i, t: (t, 0))],
        out_specs=[pl.BlockSpec((None, dq, tt), lambda i, t: (i, 0, t)),
                   pl.BlockSpec((None, tt, dkv), lambda i, t: (i, t, 0)),
                   pl.BlockSpec((None, tt // LANES, dkv, LANES), lambda i, t: (i, t, 0, 0))],
        out_shape=[jax.ShapeDtypeStruct((b, dq, l), bf16),
                   jax.ShapeDtypeStruct((b, l, dkv), bf16),
                   jax.ShapeDtypeStruct((b, l // LANES, dkv, LANES), bf16)],
        compiler_params=_cparams(2),
        name="attn_qkv",
    )(x, mx, g, wb, cos, sin)
    kc, vct = pl.pallas_call(
        functools.partial(_kv_kernel, dkv=dkv),
        grid=(b,),
        in_specs=[pl.BlockSpec((None, c, d), lambda i: (i, 0, 0)),
                  pl.BlockSpec((None, N_MOD, d), lambda i: (0, 0, 0)),
                  pl.BlockSpec(g.shape, lambda i: (0, 0)),
                  _resident((d, 2 * dkv))],
        out_specs=[pl.BlockSpec((None, c, dkv), lambda i: (i, 0, 0)),
                   pl.BlockSpec((None, dkv, c), lambda i: (i, 0, 0))],
        out_shape=[jax.ShapeDtypeStruct((b, c, dkv), bf16), jax.ShapeDtypeStruct((b, dkv, c), bf16)],
        compiler_params=_cparams(1),
        name="attn_ctx_kv",
    )(xc, mc, g, wb[:, dq:])
    tq = Q_ROWS
    assert l % tq == 0 and l >= tq + 2 * WINDOW and tq == LANES
    return pl.pallas_call(
        _attn_kernel,
        grid=(b, l // tq),
        in_specs=[pl.BlockSpec(memory_space=pltpu.SMEM),
                  pl.BlockSpec((None, tq, d), lambda i, n: (i, n, 0)),
                  pl.BlockSpec((None, dq, tq), lambda i, n: (i, 0, n)),
                  pl.BlockSpec((None, l, dkv), lambda i, n: (i, 0, 0)),
                  pl.BlockSpec((None, l // LANES, dkv, LANES), lambda i, n: (i, 0, 0, 0)),
                  pl.BlockSpec((None, c, dkv), lambda i, n: (i, 0, 0)),
                  pl.BlockSpec((None, dkv, c), lambda i, n: (i, 0, 0)),
                  pl.BlockSpec((None, N_MOD, d), lambda i, n: (i, 0, 0)),
                  pl.BlockSpec(g.shape, lambda i, n: (0, 0)),
                  _resident((dq, d))],
        out_specs=pl.BlockSpec((None, tq, d), lambda i, n: (i, n, 0)),
        out_shape=jax.ShapeDtypeStruct(x.shape, f32),
        compiler_params=_cparams(2),
        name="attn_core",
    )(sink, x, qt, k, vt, kc, vct, mx, g, w_o.astype(bf16))


def kernel(x, c, ctx, c_ctx, w_mod, b_mod, norm_g, ffn_w_in, ffn_w_out, rg_w_in, rg_conv_w, rg_conv_b,
           rg_gate_w, rg_gate_b, rg_lambda, rg_w_out, attn_w_qkv, attn_w_o, attn_sink):
    b, l, d = x.shape
    n_ctx = ctx.shape[1]
    depth = w_mod.shape[0]
    assert depth == 2, "layer 0 = RG-LRU with context output, layer 1 = windowed attention (last)"
    rows = b + 1
    rows_pad = -(-rows // 16) * 16
    cond = jnp.zeros((rows_pad, d), f32).at[:b].set(c).at[b].set(c_ctx)
    mods = _modulation(cond, w_mod, b_mod)
    xc = ctx
    for i in range(depth):
        mx, mc, g = mods[i, :b], mods[i, b:b + 1], norm_g[i]
        flat = lambda a: a.reshape(1, b * n_ctx, d)
        w1 = _ffn_weights(ffn_w_in[i, 0], ffn_w_out[i, 0])
        w2 = _ffn_weights(ffn_w_in[i, 1], ffn_w_out[i, 1])
        x = _ffn(x, mx, g, w1, 0)
        xc = _ffn(flat(xc), mc, g, w1, 0).reshape(b, n_ctx, d)
        if i == 0:
            x, xc = _rglru(x, xc, mx, mc, g, rg_w_in[0], rg_conv_w[0], rg_conv_b[0], rg_gate_w[0],
                           rg_gate_b[0], rg_lambda[0], rg_w_out[0])
            x = _ffn(x, mx, g, w2, 2)
            xc = _ffn(flat(xc), mc, g, w2, 2).reshape(b, n_ctx, d)
        else:
            x = _attention(x, xc, mx, mc, g, attn_w_qkv[0], attn_w_o[0], attn_sink[0])
            x = _ffn(x, mx, g, w2, 2)
    return x
```

```python
import functools

import jax
import jax.numpy as jnp
from jax import lax
from jax.experimental import pallas as pl
from jax.experimental.pallas import tpu as pltpu

f32 = jnp.float32
bf16 = jnp.bfloat16

N_MOD = 9
FFN_RES_WEIGHT = 0.5
NORM_EPS = 1e-6
NEG_INF = -1e30
RG_BLOCKS = 4
CONV_W = 4
LRU_C = 8.0
HEAD_DIM = 64
N_KV_HEADS = 4
WINDOW = 128
GRID_W = 64
ROPE_BASE = 10000.0
ROPE_AXIS_DIM = HEAD_DIM // 2
LOG2E = 1.4426950408889634
Q_SCALE = HEAD_DIM ** -0.5 * LOG2E

LANES = 128
SUBLANES = 8
VMEM_LIMIT = 56 * 1024 * 1024
FFN_CHUNK = 256
FFN_ROWS = 512
MIX_ROWS = 256
Q_ROWS = 128


def _cparams(n_axes):
    return pltpu.CompilerParams(dimension_semantics=("arbitrary",) * n_axes,
                                vmem_limit_bytes=VMEM_LIMIT)


def _resident(shape):
    return pl.BlockSpec(shape, lambda *_: (0,) * len(shape), pipeline_mode=pl.Buffered(1))


def _rms(x, gain):
    ms = jnp.mean(x * x, axis=-1, keepdims=True)
    return x * lax.rsqrt(ms + NORM_EPS) * gain


def _modulate(x, g, shift, scale):
    return _rms(x, g * (1.0 + scale)) + shift


def _mod_kernel(cond_ref, w_ref, b_ref, o_ref):
    c = cond_ref[...]
    s = (c * jax.nn.sigmoid(c)).astype(bf16)
    o_ref[...] = jnp.dot(s, w_ref[...].astype(bf16), preferred_element_type=f32) + b_ref[...]


def _modulation(cond, w_mod, b_mod):
    depth, d, n = w_mod.shape
    r = cond.shape[0]
    tn = 1536
    out = pl.pallas_call(
        _mod_kernel,
        grid=(depth, n // tn),
        in_specs=[pl.BlockSpec((r, d), lambda i, j: (0, 0)),
                  pl.BlockSpec((None, d, tn), lambda i, j: (i, 0, j)),
                  pl.BlockSpec((None, 1, tn), lambda i, j: (i, 0, j))],
        out_specs=pl.BlockSpec((None, r, tn), lambda i, j: (i, 0, j)),
        out_shape=jax.ShapeDtypeStruct((depth, r, n), f32),
        compiler_params=_cparams(2),
        name="modulation",
    )(cond, w_mod, b_mod.reshape(depth, 1, n))
    return out.reshape(depth, r, N_MOD, d)


def _ffn_kernel(x_ref, mod_ref, g_ref, win_ref, wout_ref, o_ref, *, k):
    x = x_ref[...]
    n_chunks, _, two_tf = win_ref.shape
    tf = two_tf // 2
    h = _modulate(x, g_ref[k:k + 1], mod_ref[3 * k:3 * k + 1], mod_ref[3 * k + 1:3 * k + 2])
    hb = h.astype(bf16)

    def hidden(j):
        gu = jnp.dot(hb, win_ref[j], preferred_element_type=f32)
        gate = gu[:, :tf]
        return (gate * jax.nn.sigmoid(gate) * gu[:, tf:]).astype(bf16)

    a_prev = hidden(0)
    y = None
    for j in range(1, n_chunks + 1):
        part = jnp.dot(a_prev, wout_ref[j - 1], preferred_element_type=f32)
        y = part if y is None else y + part
        if j < n_chunks:
            a_prev = hidden(j)
    o_ref[...] = x + _rms(y, FFN_RES_WEIGHT * mod_ref[3 * k + 2:3 * k + 3] * g_ref[3 + k:4 + k])


def _ffn_weights(w_in, w_out):
    d, two_f = w_in.shape
    f = two_f // 2
    n = f // FFN_CHUNK
    wi = w_in.astype(bf16).reshape(d, 2, n, FFN_CHUNK)
    wi = jnp.transpose(wi, (2, 0, 1, 3)).reshape(n, d, 2 * FFN_CHUNK)
    return wi, w_out.astype(bf16).reshape(n, FFN_CHUNK, d)


def _ffn(x, mod, g, weights, k):
    win, wout = weights
    bn, ln, d = x.shape
    tm = min(FFN_ROWS, ln)
    assert ln % tm == 0
    return pl.pallas_call(
        functools.partial(_ffn_kernel, k=k),
        grid=(bn, ln // tm),
        in_specs=[pl.BlockSpec((None, tm, d), lambda b, t: (b, t, 0)),
                  pl.BlockSpec((None, N_MOD, d), lambda b, t: (b, 0, 0)),
                  pl.BlockSpec(g.shape, lambda b, t: (0, 0)),
                  _resident(win.shape),
                  _resident(wout.shape)],
        out_specs=pl.BlockSpec((None, tm, d), lambda b, t: (b, t, 0)),
        out_shape=jax.ShapeDtypeStruct(x.shape, f32),
        compiler_params=_cparams(2),
        name=f"ffn{k}",
    )(x, mod, g, win, wout)


def _segment_scan(a, b, c0, reverse):
    t, w = a.shape
    n = t // SUBLANES
    h_loc = [None] * n
    a_cum = [None] * n
    h = acc = None
    for k in (range(n - 1, -1, -1) if reverse else range(n)):
        ak, bk = a[k * SUBLANES:(k + 1) * SUBLANES], b[k * SUBLANES:(k + 1) * SUBLANES]
        h, acc = (bk, ak) if h is None else (ak * h + bk, ak * acc)
        h_loc[k], a_cum[k] = h, acc
    row = lax.broadcasted_iota(jnp.int32, (SUBLANES, w), 0)
    for j in (1, 2, 4):
        shift = SUBLANES - j if reverse else j
        m = (row < SUBLANES - j) if reverse else (row >= j)
        h_sh = pltpu.roll(h, shift, axis=0)
        a_sh = pltpu.roll(acc, shift, axis=0)
        h = jnp.where(m, acc * h_sh + h, h)
        acc = jnp.where(m, acc * a_sh, acc)
    end = acc * c0 + h
    entry = SUBLANES - 1 if reverse else 0
    c_in = jnp.where(row == entry, c0, pltpu.roll(end, SUBLANES - 1 if reverse else 1, axis=0))
    out = jnp.concatenate([h_loc[k] + a_cum[k] * c_in for k in range(n)], axis=0)
    last = 0 if reverse else SUBLANES - 1
    return out, end[last:last + 1]


def _rg_kernel(x_ref, c_ref, mx_ref, mc_ref, g_ref, perm_ref, permt_ref, win_ref, cw_ref, cb_ref, gw_ref,
               gb_ref, lam_ref, wout_ref, xo_ref, co_ref, xr_s, rx_s, gx_s, sf_s, h_s, *, nx):
    tt, d = x_ref.shape
    bw = d // RG_BLOCKS
    p = pl.program_id(1)
    s = pl.program_id(2)
    is_ctx = s == 0
    mod = jnp.where(is_ctx, mc_ref[...], mx_ref[...])
    g8 = SUBLANES
    front = 2 * g8
    back = g8

    @pl.when(p == 0)
    def _project():
        @pl.when(is_ctx)
        def _():
            xr_s[0:front, :] = jnp.zeros((front, d), f32)
            xr_s[front + (nx + 1) * tt:front + back + (nx + 1) * tt, :] = jnp.zeros((back, d), f32)
        xt = jnp.where(is_ctx, c_ref[...], x_ref[...])
        h = _modulate(xt, g_ref[1:2], mod[3:4], mod[4:5]).astype(bf16)
        hp = jnp.dot(perm_ref[...], h, preferred_element_type=f32).astype(bf16)
        gr = jnp.dot(hp, win_ref[...], preferred_element_type=f32)
        base = pl.multiple_of(s * tt, tt)
        gx_s[pl.ds(base, tt), :] = jax.nn.gelu(gr[:, :d]).astype(bf16)
        xr_s[pl.ds(base + front, tt), :] = gr[:, d:]

    def coeffs(rx, direction, blk):
        ri = jnp.dot(rx.astype(bf16), gw_ref[direction, blk], preferred_element_type=f32)
        cols = slice(blk * bw, (blk + 1) * bw)
        r = jax.nn.sigmoid(ri[:, :bw] + gb_ref[2 * direction:2 * direction + 1, cols])
        i = jax.nn.sigmoid(ri[:, bw:] + gb_ref[2 * direction + 1:2 * direction + 2, cols])
        lam = lam_ref[direction:direction + 1, cols]
        a = jnp.exp2(r * (-LRU_C * LOG2E * jax.nn.softplus(-lam)))
        v = 1.0 - a * a
        root = jnp.where(v > 0.0, v * lax.rsqrt(v), 0.0)
        return a, root * (i * rx)

    @pl.when(p == 1)
    def _forward():
        base = pl.multiple_of(s * tt, tt)
        has_prev = s >= 2
        has_next = jnp.logical_and(s >= 1, s < nx)
        row = lax.broadcasted_iota(jnp.int32, (g8, bw), 0)
        for blk in range(RG_BLOCKS):
            cols = slice(blk * bw, (blk + 1) * bw)
            cur = xr_s[pl.ds(base + front, tt), cols]
            prev2 = jnp.where(has_prev, xr_s[pl.ds(base + front - 2 * g8, g8), cols], 0.0)
            prev1 = jnp.where(has_prev, xr_s[pl.ds(base + front - g8, g8), cols], 0.0)
            nxt = jnp.where(has_next, xr_s[pl.ds(base + front + tt, g8), cols], 0.0)
            m1 = jnp.where(row == 0, pltpu.roll(prev1, 1, axis=0), pltpu.roll(cur[tt - g8:], 1, axis=0))
            m2 = jnp.where(row == 0, pltpu.roll(prev2, 1, axis=0),
                           pltpu.roll(cur[tt - 2 * g8:tt - g8], 1, axis=0))
            p1 = jnp.where(row == g8 - 1, pltpu.roll(nxt, g8 - 1, axis=0), pltpu.roll(cur[:g8], g8 - 1, axis=0))
            rx = (cb_ref[0:1, cols]
                  + cw_ref[0:1, cols] * jnp.concatenate([m2, m1, cur[:tt - 2 * g8]], axis=0)
                  + cw_ref[1:2, cols] * jnp.concatenate([m1, cur[:tt - g8]], axis=0)
                  + cw_ref[2:3, cols] * cur
                  + cw_ref[3:4, cols] * jnp.concatenate([cur[g8:], p1], axis=0))
            rx_s[pl.ds(base, tt), cols] = rx
            a, b = coeffs(rx, 0, blk)
            h0 = jnp.where(is_ctx, 0.0, h_s[0:1, cols])
            hf, h_out = _segment_scan(a, b, h0, reverse=False)
            sf_s[pl.ds(base, tt), cols] = hf.astype(bf16)
            h_s[0:1, cols] = h_out

    @pl.when(p == 2)
    def _backward():
        tile = jnp.where(is_ctx, 0, nx + 1 - s)
        base = pl.multiple_of(tile * tt, tt)
        zs = []
        for blk in range(RG_BLOCKS):
            cols = slice(blk * bw, (blk + 1) * bw)
            rx = rx_s[pl.ds(base, tt), cols]
            a, b = coeffs(rx, 1, blk)
            h0 = jnp.where(is_ctx, 0.0, h_s[1:2, cols])
            hb, h_out = _segment_scan(a, b, h0, reverse=True)
            h_s[1:2, cols] = h_out
            tot = sf_s[pl.ds(base, tt), cols].astype(f32) + hb
            zs.append((gx_s[pl.ds(base, tt), cols].astype(f32) * tot).astype(bf16))
        z = jnp.dot(permt_ref[...], jnp.concatenate(zs, axis=1), preferred_element_type=f32)
        y = jnp.dot(z.astype(bf16), wout_ref[...], preferred_element_type=f32)
        upd = _rms(y, mod[5:6] * g_ref[4:5])

        @pl.when(is_ctx)
        def _():
            co_ref[...] = c_ref[...] + upd

        @pl.when(jnp.logical_not(is_ctx))
        def _():
            xo_ref[...] = x_ref[...] + upd


def _rglru(x, xc, mx, mc, g, w_in, conv_w, conv_b, gate_w, gate_b, lam, w_out):
    b, l, d = x.shape
    c = xc.shape[1]
    tt = MIX_ROWS
    assert c == tt and l % tt == 0, "context must be exactly one mixer tile"
    nx = l // tt
    bw = d // RG_BLOCKS
    gw = jnp.concatenate([gate_w[:, 0], gate_w[:, 1]], axis=-1).astype(bf16)
    gb = gate_b.reshape(4, d)
    rows = (nx + 1) * tt
    dst = jnp.arange(tt)
    src = (dst % SUBLANES) * (tt // SUBLANES) + dst // SUBLANES
    perm = (src[:, None] == jnp.arange(tt)[None, :]).astype(bf16)

    def x_tile(p, s):
        return jnp.where(p == 0, jnp.maximum(s - 1, 0), jnp.where(p == 1, nx - 1, nx - jnp.maximum(s, 1)))

    def xo_tile(p, s):
        return jnp.where(p == 2, nx - jnp.maximum(s, 1), nx - 1)

    return pl.pallas_call(
        functools.partial(_rg_kernel, nx=nx),
        grid=(b, 3, nx + 1),
        in_specs=[pl.BlockSpec((None, tt, d), lambda i, p, s: (i, x_tile(p, s), 0)),
                  pl.BlockSpec((None, tt, d), lambda i, p, s: (i, 0, 0)),
                  pl.BlockSpec((None, N_MOD, d), lambda i, p, s: (i, 0, 0)),
                  pl.BlockSpec((None, N_MOD, d), lambda i, p, s: (0, 0, 0)),
                  pl.BlockSpec(g.shape, lambda i, p, s: (0, 0)),
                  _resident((tt, tt)),
                  _resident((tt, tt)),
                  _resident((d, 2 * d)),
                  pl.BlockSpec(conv_w.shape, lambda i, p, s: (0, 0)),
                  pl.BlockSpec((1, d), lambda i, p, s: (0, 0)),
                  _resident(gw.shape),
                  pl.BlockSpec(gb.shape, lambda i, p, s: (0, 0)),
                  pl.BlockSpec(lam.shape, lambda i, p, s: (0, 0)),
                  _resident((d, d))],
        out_specs=[pl.BlockSpec((None, tt, d), lambda i, p, s: (i, xo_tile(p, s), 0)),
                   pl.BlockSpec((None, tt, d), lambda i, p, s: (i, 0, 0))],
        out_shape=[jax.ShapeDtypeStruct(x.shape, f32), jax.ShapeDtypeStruct(xc.shape, f32)],
        scratch_shapes=[pltpu.VMEM((rows + 3 * SUBLANES, d), f32),
                        pltpu.VMEM((rows, d), f32),
                        pltpu.VMEM((rows, d), bf16),
                        pltpu.VMEM((rows, d), bf16),
                        pltpu.VMEM((SUBLANES, d), f32)],
        compiler_params=_cparams(3),
        name="rglru",
    )(x, xc, mx, mc, g, perm, perm.T, w_in.astype(bf16), conv_w, conv_b.reshape(1, d), gw, gb, lam,
      w_out.astype(bf16))


def _rope_tables(l):
    rows = l // GRID_W
    row = jnp.repeat(jnp.arange(rows, dtype=f32), GRID_W)
    col = jnp.tile(jnp.arange(GRID_W, dtype=f32), rows)
    inv = 1.0 / (ROPE_BASE ** (jnp.arange(0, ROPE_AXIS_DIM, 2, dtype=f32) / ROPE_AXIS_DIM))
    ar, ac = row[:, None] * inv, col[:, None] * inv
    cos = jnp.concatenate([jnp.cos(ar), jnp.cos(ar), jnp.cos(ac), jnp.cos(ac)], axis=-1)
    sin = jnp.concatenate([-jnp.sin(ar), jnp.sin(ar), -jnp.sin(ac), jnp.sin(ac)], axis=-1)
    reps = LANES // HEAD_DIM
    return jnp.tile(cos, (1, reps)), jnp.tile(sin, (1, reps))


def _qkv_kernel(x_ref, mod_ref, g_ref, w_ref, cos_ref, sin_ref, qt_ref, k_ref, vt_ref, *, dq, dkv):
    tt = x_ref.shape[0]
    h = _modulate(x_ref[...], g_ref[1:2], mod_ref[3:4], mod_ref[4:5])
    qkv = jnp.dot(h.astype(bf16), w_ref[...], preferred_element_type=f32)
    cos = cos_ref[...]
    sin = sin_ref[...]
    lane = lax.broadcasted_iota(jnp.int32, cos.shape, 1)
    half = ROPE_AXIS_DIM // 2
    first = (lane % ROPE_AXIS_DIM) < half

    def rope(z):
        partner = jnp.where(first, pltpu.roll(z, LANES - half, axis=1), pltpu.roll(z, half, axis=1))
        return z * cos + partner * sin

    for j in range(dq // LANES):
        cols = slice(j * LANES, (j + 1) * LANES)
        qt_ref[cols, :] = (rope(qkv[:, cols]) * Q_SCALE).T.astype(bf16)
    for j in range(dkv // LANES):
        k_ref[:, j * LANES:(j + 1) * LANES] = rope(qkv[:, dq + j * LANES:dq + (j + 1) * LANES]).astype(bf16)
    vt = qkv[:, dq + dkv:].T.astype(bf16)
    for j in range(tt // LANES):
        vt_ref[j] = vt[:, j * LANES:(j + 1) * LANES]


def _kv_kernel(x_ref, mod_ref, g_ref, w_ref, k_ref, vt_ref, *, dkv):
    h = _modulate(x_ref[...], g_ref[1:2], mod_ref[3:4], mod_ref[4:5])
    kv = jnp.dot(h.astype(bf16), w_ref[...], preferred_element_type=f32)
    k_ref[...] = kv[:, :dkv].astype(bf16)
    vt_ref[...] = kv[:, dkv:].T.astype(bf16)


def _attn_kernel(sink_ref, x_ref, qt_ref, k_ref, vt_ref, kc_ref, vct_ref, mod_ref, g_ref, wo_ref, o_ref):
    tq, d = x_ref.shape
    l = k_ref.shape[0]
    span = tq + 2 * WINDOW
    group = d // HEAD_DIM // N_KV_HEADS
    lanes = group * tq
    start = pl.program_id(1) * tq
    ws = pl.multiple_of(jnp.clip(start - WINDOW, 0, l - span), LANES)
    kw = k_ref[pl.ds(ws, span), :]
    slab = ws // LANES
    vtw = jnp.concatenate([vt_ref[slab + i] for i in range(span // LANES)], axis=1)
    kc = kc_ref[...]
    vct = vct_ref[...]
    qt = qt_ref[...]
    kj = lax.broadcasted_iota(jnp.int32, (span, lanes), 0)
    qi = lax.broadcasted_iota(jnp.int32, (span, lanes), 1) % tq
    valid = jnp.abs(kj - qi + (ws - start)) <= WINDOW

    def with_ones(v):
        row = lax.broadcasted_iota(jnp.int32, (2 * SUBLANES, v.shape[1]), 0)
        return jnp.concatenate([v, (row == 0).astype(bf16)], axis=0)

    def scores(kh):
        heads = range(kh * group, (kh + 1) * group)
        qg = jnp.concatenate([qt[h * HEAD_DIM:(h + 1) * HEAD_DIM] for h in heads], axis=1)
        pair, odd = divmod(kh, LANES // HEAD_DIM)
        zero = jnp.zeros_like(qg)
        qz = jnp.concatenate([zero, qg] if odd else [qg, zero], axis=0)
        kcols = slice(pair * LANES, (pair + 1) * LANES)
        return (jnp.dot(kw[:, kcols], qz, preferred_element_type=f32),
                jnp.dot(kc[:, kcols], qz, preferred_element_type=f32))

    def weights(kh, s_lat, s_ctx):
        heads = range(kh * group, (kh + 1) * group)
        s_lat = jnp.where(valid, s_lat, NEG_INF)
        sink = jnp.concatenate([jnp.full((1, tq), sink_ref[h] * LOG2E, f32) for h in heads], axis=1)
        m = jnp.maximum(jnp.maximum(jnp.max(s_lat, axis=0, keepdims=True),
                                    jnp.max(s_ctx, axis=0, keepdims=True)), sink)
        return jnp.exp2(s_lat - m).astype(bf16), jnp.exp2(s_ctx - m).astype(bf16), jnp.exp2(sink - m)

    def mix(kh, p_lat, p_ctx, p_sink):
        vrows = slice(kh * HEAD_DIM, (kh + 1) * HEAD_DIM)
        o = (jnp.dot(with_ones(vtw[vrows]), p_lat, preferred_element_type=f32)
             + jnp.dot(with_ones(vct[vrows]), p_ctx, preferred_element_type=f32))
        o = o[:HEAD_DIM] / (o[HEAD_DIM:HEAD_DIM + 1] + p_sink)
        return [o[:, i * tq:(i + 1) * tq] for i in range(group)]

    pieces = []
    s_next = scores(0)
    p_prev = None
    for kh in range(N_KV_HEADS):
        s_cur = s_next
        if kh + 1 < N_KV_HEADS:
            s_next = scores(kh + 1)
        if p_prev is not None:
            pieces += mix(kh - 1, *p_prev)
        p_prev = weights(kh, *s_cur)
    pieces += mix(N_KV_HEADS - 1, *p_prev)
    att = jnp.concatenate(pieces, axis=0).T.astype(bf16)
    y = jnp.dot(att, wo_ref[...], preferred_element_type=f32)
    o_ref[...] = x_ref[...] + _rms(y, mod_ref[5:6] * g_ref[4:5])


def _attention(x, xc, mx, mc, g, w_qkv, w_o, sink):
    b, l, d = x.shape
    c = xc.shape[1]
    dq = w_o.shape[0]
    dkv = (w_qkv.shape[1] - dq) // 2
    tt = min(MIX_ROWS, l)
    cos, sin = _rope_tables(l)
    wb = w_qkv.astype(bf16)
    qt, k, vt = pl.pallas_call(
        functools.partial(_qkv_kernel, dq=dq, dkv=dkv),
        grid=(b, l // tt),
        in_specs=[pl.BlockSpec((None, tt, d), lambda i, t: (i, t, 0)),
                  pl.BlockSpec((None, N_MOD, d), lambda i, t: (i, 0, 0)),
                  pl.BlockSpec(g.shape, lambda i, t: (0, 0)),
                  _resident(wb.shape),
                  pl.BlockSpec((tt, LANES), lambda i, t: (t, 0)),
                  pl.BlockSpec((tt, LANES), lambda i, t: (t, 0))],
        out_specs=[pl.BlockSpec((None, dq, tt), lambda i, t: (i, 0, t)),
                   pl.BlockSpec((None, tt, dkv), lambda i, t: (i, t, 0)),
                   pl.BlockSpec((None, tt // LANES, dkv, LANES), lambda i, t: (i, t, 0, 0))],
        out_shape=[jax.ShapeDtypeStruct((b, dq, l), bf16),
                   jax.ShapeDtypeStruct((b, l, dkv), bf16),
                   jax.ShapeDtypeStruct((b, l // LANES, dkv, LANES), bf16)],
        compiler_params=_cparams(2),
        name="attn_qkv",
    )(x, mx, g, wb, cos, sin)
    kc, vct = pl.pallas_call(
        functools.partial(_kv_kernel, dkv=dkv),
        grid=(b,),
        in_specs=[pl.BlockSpec((None, c, d), lambda i: (i, 0, 0)),
                  pl.BlockSpec((None, N_MOD, d), lambda i: (0, 0, 0)),
                  pl.BlockSpec(g.shape, lambda i: (0, 0)),
                  _resident((d, 2 * dkv))],
        out_specs=[pl.BlockSpec((None, c, dkv), lambda i: (i, 0, 0)),
                   pl.BlockSpec((None, dkv, c), lambda i: (i, 0, 0))],
        out_shape=[jax.ShapeDtypeStruct((b, c, dkv), bf16), jax.ShapeDtypeStruct((b, dkv, c), bf16)],
        compiler_params=_cparams(1),
        name="attn_ctx_kv",
    )(xc, mc, g, wb[:, dq:])
    tq = Q_ROWS
    assert l % tq == 0 and l >= tq + 2 * WINDOW and tq == LANES
    return pl.pallas_call(
        _attn_kernel,
        grid=(b, l // tq),
        in_specs=[pl.BlockSpec(memory_space=pltpu.SMEM),
                  pl.BlockSpec((None, tq, d), lambda i, n: (i, n, 0)),
                  pl.BlockSpec((None, dq, tq), lambda i, n: (i, 0, n)),
                  pl.BlockSpec((None, l, dkv), lambda i, n: (i, 0, 0)),
                  pl.BlockSpec((None, l // LANES, dkv, LANES), lambda i, n: (i, 0, 0, 0)),
                  pl.BlockSpec((None, c, dkv), lambda i, n: (i, 0, 0)),
                  pl.BlockSpec((None, dkv, c), lambda i, n: (i, 0, 0)),
                  pl.BlockSpec((None, N_MOD, d), lambda i, n: (i, 0, 0)),
                  pl.BlockSpec(g.shape, lambda i, n: (0, 0)),
                  _resident((dq, d))],
        out_specs=pl.BlockSpec((None, tq, d), lambda i, n: (i, n, 0)),
        out_shape=jax.ShapeDtypeStruct(x.shape, f32),
        compiler_params=_cparams(2),
        name="attn_core",
    )(sink, x, qt, k, vt, kc, vct, mx, g, w_o.astype(bf16))


def kernel(x, c, ctx, c_ctx, w_mod, b_mod, norm_g, ffn_w_in, ffn_w_out, rg_w_in, rg_conv_w, rg_conv_b,
           rg_gate_w, rg_gate_b, rg_lambda, rg_w_out, attn_w_qkv, attn_w_o, attn_sink):
    b, l, d = x.shape
    n_ctx = ctx.shape[1]
    depth = w_mod.shape[0]
    assert depth == 2, "layer 0 = RG-LRU with context output, layer 1 = windowed attention (last)"
    rows = b + 1
    rows_pad = -(-rows // 16) * 16
    cond = jnp.zeros((rows_pad, d), f32).at[:b].set(c).at[b].set(c_ctx)
    mods = _modulation(cond, w_mod, b_mod)
    xc = ctx
    for i in range(depth):
        mx, mc, g = mods[i, :b], mods[i, b:b + 1], norm_g[i]
        flat = lambda a: a.reshape(1, b * n_ctx, d)
        w1 = _ffn_weights(ffn_w_in[i, 0], ffn_w_out[i, 0])
        w2 = _ffn_weights(ffn_w_in[i, 1], ffn_w_out[i, 1])
        x = _ffn(x, mx, g, w1, 0)
        xc = _ffn(flat(xc), mc, g, w1, 0).reshape(b, n_ctx, d)
        if i == 0:
            x, xc = _rglru(x, xc, mx, mc, g, rg_w_in[0], rg_conv_w[0], rg_conv_b[0], rg_gate_w[0],
                           rg_gate_b[0], rg_lambda[0], rg_w_out[0])
            x = _ffn(x, mx, g, w2, 2)
            xc = _ffn(flat(xc), mc, g, w2, 2).reshape(b, n_ctx, d)
        else:
            x = _attention(x, xc, mx, mc, g, attn_w_qkv[0], attn_w_o[0], attn_sink[0])
            x = _ffn(x, mx, g, w2, 2)
    return x
```

```python
import functools

import jax
import jax.numpy as jnp
from jax import lax
from jax.experimental import pallas as pl
from jax.experimental.pallas import tpu as pltpu

f32 = jnp.float32
bf16 = jnp.bfloat16

N_MOD = 9
FFN_RES_WEIGHT = 0.5
NORM_EPS = 1e-6
NEG_INF = -1e30
RG_BLOCKS = 4
CONV_W = 4
LRU_C = 8.0
HEAD_DIM = 64
N_KV_HEADS = 4
WINDOW = 128
GRID_W = 64
ROPE_BASE = 10000.0
ROPE_AXIS_DIM = HEAD_DIM // 2
LOG2E = 1.4426950408889634
Q_SCALE = HEAD_DIM ** -0.5 * LOG2E

LANES = 128
SUBLANES = 8
VMEM_LIMIT = 56 * 1024 * 1024
FFN_CHUNK = 256
FFN_ROWS = 512
MIX_ROWS = 256
RG_TILES_PER_STEP = 2
Q_ROWS = 128
Q_TILES_PER_STEP = 2


def _cparams(n_axes):
    return pltpu.CompilerParams(dimension_semantics=("arbitrary",) * n_axes,
                                vmem_limit_bytes=VMEM_LIMIT)


def _resident(shape):
    return pl.BlockSpec(shape, lambda *_: (0,) * len(shape), pipeline_mode=pl.Buffered(1))


def _rms(x, gain):
    ms = jnp.mean(x * x, axis=-1, keepdims=True)
    return x * lax.rsqrt(ms + NORM_EPS) * gain


def _modulate(x, g, shift, scale):
    return _rms(x, g * (1.0 + scale)) + shift


def _mod_kernel(cond_ref, w_ref, b_ref, o_ref):
    c = cond_ref[...]
    s = (c * jax.nn.sigmoid(c)).astype(bf16)
    o_ref[...] = jnp.dot(s, w_ref[...].astype(bf16), preferred_element_type=f32) + b_ref[...]


def _modulation(cond, w_mod, b_mod):
    depth, d, n = w_mod.shape
    r = cond.shape[0]
    tn = 1536
    out = pl.pallas_call(
        _mod_kernel,
        grid=(depth, n // tn),
        in_specs=[pl.BlockSpec((r, d), lambda i, j: (0, 0)),
                  pl.BlockSpec((None, d, tn), lambda i, j: (i, 0, j)),
                  pl.BlockSpec((None, 1, tn), lambda i, j: (i, 0, j))],
        out_specs=pl.BlockSpec((None, r, tn), lambda i, j: (i, 0, j)),
        out_shape=jax.ShapeDtypeStruct((depth, r, n), f32),
        compiler_params=_cparams(2),
        name="modulation",
    )(cond, w_mod, b_mod.reshape(depth, 1, n))
    return out.reshape(depth, r, N_MOD, d)


def _ffn_kernel(x_ref, mod_ref, g_ref, win_ref, wout_ref, o_ref, *, k):
    x = x_ref[...]
    f = wout_ref.shape[0]
    tf = FFN_CHUNK
    n_chunks = f // tf
    h = _modulate(x, g_ref[k:k + 1], mod_ref[3 * k:3 * k + 1], mod_ref[3 * k + 1:3 * k + 2])
    hb = h.astype(bf16)

    def hidden(j):
        gate = jnp.dot(hb, win_ref[:, j * tf:(j + 1) * tf], preferred_element_type=f32)
        up = jnp.dot(hb, win_ref[:, f + j * tf:f + (j + 1) * tf], preferred_element_type=f32)
        return (gate * jax.nn.sigmoid(gate) * up).astype(bf16)

    a_prev = hidden(0)
    y = None
    for j in range(1, n_chunks + 1):
        part = jnp.dot(a_prev, wout_ref[(j - 1) * tf:j * tf, :], preferred_element_type=f32)
        y = part if y is None else y + part
        if j < n_chunks:
            a_prev = hidden(j)
    o_ref[...] = x + _rms(y, FFN_RES_WEIGHT * mod_ref[3 * k + 2:3 * k + 3] * g_ref[3 + k:4 + k])


def _ffn_weights(w_in, w_out):
    assert w_out.shape[0] % FFN_CHUNK == 0
    return w_in.astype(bf16), w_out.astype(bf16)


def _ffn(x, mod, g, weights, k):
    win, wout = weights
    bn, ln, d = x.shape
    tm = min(FFN_ROWS, ln)
    assert ln % tm == 0
    return pl.pallas_call(
        functools.partial(_ffn_kernel, k=k),
        grid=(bn, ln // tm),
        in_specs=[pl.BlockSpec((None, tm, d), lambda b, t: (b, t, 0)),
                  pl.BlockSpec((None, N_MOD, d), lambda b, t: (b, 0, 0)),
                  pl.BlockSpec(g.shape, lambda b, t: (0, 0)),
                  _resident(win.shape),
                  _resident(wout.shape)],
        out_specs=pl.BlockSpec((None, tm, d), lambda b, t: (b, t, 0)),
        out_shape=jax.ShapeDtypeStruct(x.shape, f32),
        compiler_params=_cparams(2),
        name=f"ffn{k}",
    )(x, mod, g, win, wout)


def _segment_scan(a, b, c0, reverse):
    t, w = a.shape
    n = t // SUBLANES
    h_loc = [None] * n
    a_cum = [None] * n
    h = acc = None
    for k in (range(n - 1, -1, -1) if reverse else range(n)):
        ak, bk = a[k * SUBLANES:(k + 1) * SUBLANES], b[k * SUBLANES:(k + 1) * SUBLANES]
        h, acc = (bk, ak) if h is None else (ak * h + bk, ak * acc)
        h_loc[k], a_cum[k] = h, acc
    row = lax.broadcasted_iota(jnp.int32, (SUBLANES, w), 0)
    for j in (1, 2, 4):
        shift = SUBLANES - j if reverse else j
        m = (row < SUBLANES - j) if reverse else (row >= j)
        h_sh = pltpu.roll(h, shift, axis=0)
        a_sh = pltpu.roll(acc, shift, axis=0)
        h = jnp.where(m, acc * h_sh + h, h)
        acc = jnp.where(m, acc * a_sh, acc)
    end = acc * c0 + h
    entry = SUBLANES - 1 if reverse else 0
    c_in = jnp.where(row == entry, c0, pltpu.roll(end, SUBLANES - 1 if reverse else 1, axis=0))
    out = jnp.concatenate([h_loc[k] + a_cum[k] * c_in for k in range(n)], axis=0)
    last = 0 if reverse else SUBLANES - 1
    return out, end[last:last + 1]


def _rg_kernel(x_ref, c_ref, mx_ref, mc_ref, g_ref, perm_ref, permt_ref, win_ref, cw_ref, cb_ref, gw_ref,
               gb_ref, lam_ref, wout_ref, xo_ref, co_ref, xr_s, rx_s, gx_s, sf_s, h_s, *, nx):
    tt, d = c_ref.shape
    n_sub = x_ref.shape[0] // tt
    n_steps = nx // n_sub
    bw = d // RG_BLOCKS
    p = pl.program_id(1)
    s = pl.program_id(2)
    is_ctx = s == 0
    is_lat = s > 0
    g8 = SUBLANES
    front = 2 * g8
    back = g8

    def sub_rows(i):
        return slice(i * tt, (i + 1) * tt)

    def tile_base(tile):
        return tile * tt if isinstance(tile, int) else pl.multiple_of(tile * tt, tt)

    def select(flag, v):
        return (v if flag else jnp.zeros_like(v)) if isinstance(flag, bool) else jnp.where(flag, v, 0.0)

    def project(xt, mod_ref, tile):
        h = _modulate(xt, g_ref[1:2], mod_ref[3:4], mod_ref[4:5]).astype(bf16)
        hp = jnp.dot(perm_ref[...], h, preferred_element_type=f32).astype(bf16)
        gr = jnp.dot(hp, win_ref[...], preferred_element_type=f32)
        base = tile_base(tile)
        gx_s[pl.ds(base, tt), :] = jax.nn.gelu(gr[:, :d]).astype(bf16)
        xr_s[pl.ds(base + front, tt), :] = gr[:, d:]

    @pl.when(jnp.logical_and(p == 0, is_ctx))
    def _():
        xr_s[0:front, :] = jnp.zeros((front, d), f32)
        xr_s[front + (nx + 1) * tt:front + back + (nx + 1) * tt, :] = jnp.zeros((back, d), f32)
        project(c_ref[...], mc_ref, 0)

    @pl.when(jnp.logical_and(p == 0, is_lat))
    def _():
        for i in range(n_sub):
            project(x_ref[sub_rows(i), :], mx_ref, 1 + (s - 1) * n_sub + i)

    def coeffs(rx, direction, blk):
        ri = jnp.dot(rx.astype(bf16), gw_ref[direction, blk], preferred_element_type=f32)
        cols = slice(blk * bw, (blk + 1) * bw)
        r = 1.0 / (1.0 + jnp.exp2(ri[:, :bw] + gb_ref[2 * direction:2 * direction + 1, cols]))
        i = 1.0 / (1.0 + jnp.exp2(ri[:, bw:] + gb_ref[2 * direction + 1:2 * direction + 2, cols]))
        lam = lam_ref[direction:direction + 1, cols]
        a = jnp.exp2(r * (-LRU_C * LOG2E * jax.nn.softplus(-lam)))
        v = 1.0 - a * a
        root = jnp.where(v > 0.0, v * lax.rsqrt(v), 0.0)
        return a, root * (i * rx)

    def forward(tile, from_zero, has_prev, has_next):
        base = tile_base(tile)
        row = lax.broadcasted_iota(jnp.int32, (g8, bw), 0)
        for blk in range(RG_BLOCKS):
            cols = slice(blk * bw, (blk + 1) * bw)
            cur = xr_s[pl.ds(base + front, tt), cols]
            prev2 = select(has_prev, xr_s[pl.ds(base + front - 2 * g8, g8), cols])
            prev1 = select(has_prev, xr_s[pl.ds(base + front - g8, g8), cols])
            nxt = select(has_next, xr_s[pl.ds(base + front + tt, g8), cols])
            m1 = jnp.where(row == 0, pltpu.roll(prev1, 1, axis=0), pltpu.roll(cur[tt - g8:], 1, axis=0))
            m2 = jnp.where(row == 0, pltpu.roll(prev2, 1, axis=0),
                           pltpu.roll(cur[tt - 2 * g8:tt - g8], 1, axis=0))
            p1 = jnp.where(row == g8 - 1, pltpu.roll(nxt, g8 - 1, axis=0), pltpu.roll(cur[:g8], g8 - 1, axis=0))
            rx = (cb_ref[0:1, cols]
                  + cw_ref[0:1, cols] * jnp.concatenate([m2, m1, cur[:tt - 2 * g8]], axis=0)
                  + cw_ref[1:2, cols] * jnp.concatenate([m1, cur[:tt - g8]], axis=0)
                  + cw_ref[2:3, cols] * cur
                  + cw_ref[3:4, cols] * jnp.concatenate([cur[g8:], p1], axis=0))
            rx_s[pl.ds(base, tt), cols] = rx
            a, b = coeffs(rx, 0, blk)
            h0 = jnp.zeros((1, bw), f32) if from_zero else h_s[0:1, cols]
            hf, h_out = _segment_scan(a, b, h0, reverse=False)
            sf_s[pl.ds(base, tt), cols] = hf.astype(bf16)
            h_s[0:1, cols] = h_out

    @pl.when(jnp.logical_and(p == 1, is_ctx))
    def _():
        forward(0, True, False, False)

    @pl.when(jnp.logical_and(p == 1, is_lat))
    def _():
        for i in range(n_sub):
            forward(1 + (s - 1) * n_sub + i, False,
                    True if i > 0 else s >= 2, True if i < n_sub - 1 else s < n_steps)

    def backward(tile, from_zero, mod_ref):
        base = tile_base(tile)
        zs = []
        for blk in range(RG_BLOCKS):
            cols = slice(blk * bw, (blk + 1) * bw)
            rx = rx_s[pl.ds(base, tt), cols]
            a, b = coeffs(rx, 1, blk)
            h0 = jnp.zeros((1, bw), f32) if from_zero else h_s[1:2, cols]
            hb, h_out = _segment_scan(a, b, h0, reverse=True)
            h_s[1:2, cols] = h_out
            tot = sf_s[pl.ds(base, tt), cols].astype(f32) + hb
            zs.append((gx_s[pl.ds(base, tt), cols].astype(f32) * tot).astype(bf16))
        z = jnp.dot(permt_ref[...], jnp.concatenate(zs, axis=1), preferred_element_type=f32)
        y = jnp.dot(z.astype(bf16), wout_ref[...], preferred_element_type=f32)
        return _rms(y, mod_ref[5:6] * g_ref[4:5])

    @pl.when(jnp.logical_and(p == 2, is_ctx))
    def _():
        co_ref[...] = c_ref[...] + backward(0, True, mc_ref)

    @pl.when(jnp.logical_and(p == 2, is_lat))
    def _():
        for i in reversed(range(n_sub)):
            upd = backward(1 + (n_steps - s) * n_sub + i, False, mx_ref)
            xo_ref[sub_rows(i), :] = x_ref[sub_rows(i), :] + upd


def _rglru(x, xc, mx, mc, g, w_in, conv_w, conv_b, gate_w, gate_b, lam, w_out):
    b, l, d = x.shape
    c = xc.shape[1]
    tt = MIX_ROWS
    n_sub = RG_TILES_PER_STEP
    assert c == tt and l % (n_sub * tt) == 0, "context must be exactly one mixer tile"
    nx = l // tt
    n_steps = nx // n_sub
    gw = (-LOG2E * jnp.concatenate([gate_w[:, 0], gate_w[:, 1]], axis=-1)).astype(bf16)
    gb = -LOG2E * gate_b.reshape(4, d)
    rows = (nx + 1) * tt
    dst = jnp.arange(tt)
    src = (dst % SUBLANES) * (tt // SUBLANES) + dst // SUBLANES
    perm = (src[:, None] == jnp.arange(tt)[None, :]).astype(bf16)

    def x_tile(p, s):
        return jnp.where(p == 0, jnp.maximum(s - 1, 0),
                         jnp.where(p == 1, n_steps - 1, n_steps - jnp.maximum(s, 1)))

    def xo_tile(p, s):
        return jnp.where(p == 2, n_steps - jnp.maximum(s, 1), n_steps - 1)

    return pl.pallas_call(
        functools.partial(_rg_kernel, nx=nx),
        grid=(b, 3, n_steps + 1),
        in_specs=[pl.BlockSpec((None, n_sub * tt, d), lambda i, p, s: (i, x_tile(p, s), 0)),
                  pl.BlockSpec((None, tt, d), lambda i, p, s: (i, 0, 0)),
                  pl.BlockSpec((None, N_MOD, d), lambda i, p, s: (i, 0, 0)),
                  pl.BlockSpec((None, N_MOD, d), lambda i, p, s: (0, 0, 0)),
                  pl.BlockSpec(g.shape, lambda i, p, s: (0, 0)),
                  _resident((tt, tt)),
                  _resident((tt, tt)),
                  _resident((d, 2 * d)),
                  pl.BlockSpec(conv_w.shape, lambda i, p, s: (0, 0)),
                  pl.BlockSpec((1, d), lambda i, p, s: (0, 0)),
                  _resident(gw.shape),
                  pl.BlockSpec(gb.shape, lambda i, p, s: (0, 0)),
                  pl.BlockSpec(lam.shape, lambda i, p, s: (0, 0)),
                  _resident((d, d))],
        out_specs=[pl.BlockSpec((None, n_sub * tt, d), lambda i, p, s: (i, xo_tile(p, s), 0)),
                   pl.BlockSpec((None, tt, d), lambda i, p, s: (i, 0, 0))],
        out_shape=[jax.ShapeDtypeStruct(x.shape, f32), jax.ShapeDtypeStruct(xc.shape, f32)],
        scratch_shapes=[pltpu.VMEM((rows + 3 * SUBLANES, d), f32),
                        pltpu.VMEM((rows, d), f32),
                        pltpu.VMEM((rows, d), bf16),
                        pltpu.VMEM((rows, d), bf16),
                        pltpu.VMEM((SUBLANES, d), f32)],
        compiler_params=_cparams(3),
        name="rglru",
    )(x, xc, mx, mc, g, perm, perm.T, w_in.astype(bf16), conv_w, conv_b.reshape(1, d), gw, gb, lam,
      w_out.astype(bf16))


def _rope_tables(l):
    rows = l // GRID_W
    row = jnp.repeat(jnp.arange(rows, dtype=f32), GRID_W)
    col = jnp.tile(jnp.arange(GRID_W, dtype=f32), rows)
    inv = 1.0 / (ROPE_BASE ** (jnp.arange(0, ROPE_AXIS_DIM, 2, dtype=f32) / ROPE_AXIS_DIM))
    ar, ac = row[:, None] * inv, col[:, None] * inv
    cos = jnp.concatenate([jnp.cos(ar), jnp.cos(ar), jnp.cos(ac), jnp.cos(ac)], axis=-1)
    sin = jnp.concatenate([-jnp.sin(ar), jnp.sin(ar), -jnp.sin(ac), jnp.sin(ac)], axis=-1)
    reps = LANES // HEAD_DIM
    return jnp.tile(cos, (1, reps)), jnp.tile(sin, (1, reps))


def _qkv_kernel(x_ref, mod_ref, g_ref, w_ref, cos_ref, sin_ref, qt_ref, k_ref, vt_ref, *, dq, dkv):
    tt = x_ref.shape[0]
    h = _modulate(x_ref[...], g_ref[1:2], mod_ref[3:4], mod_ref[4:5])
    qkv = jnp.dot(h.astype(bf16), w_ref[...], preferred_element_type=f32)
    cos = cos_ref[...]
    sin = sin_ref[...]
    lane = lax.broadcasted_iota(jnp.int32, cos.shape, 1)
    half = ROPE_AXIS_DIM // 2
    first = (lane % ROPE_AXIS_DIM) < half

    def rope(z):
        partner = jnp.where(first, pltpu.roll(z, LANES - half, axis=1), pltpu.roll(z, half, axis=1))
        return z * cos + partner * sin

    for j in range(dq // LANES):
        cols = slice(j * LANES, (j + 1) * LANES)
        qt_ref[cols, :] = (rope(qkv[:, cols]) * Q_SCALE).T.astype(bf16)
    for j in range(dkv // LANES):
        k_ref[:, j * LANES:(j + 1) * LANES] = rope(qkv[:, dq + j * LANES:dq + (j + 1) * LANES]).astype(bf16)
    vt = qkv[:, dq + dkv:].T.astype(bf16)
    for j in range(tt // LANES):
        vt_ref[j] = vt[:, j * LANES:(j + 1) * LANES]


def _kv_kernel(x_ref, mod_ref, g_ref, w_ref, k_ref, vt_ref, *, dkv):
    h = _modulate(x_ref[...], g_ref[1:2], mod_ref[3:4], mod_ref[4:5])
    kv = jnp.dot(h.astype(bf16), w_ref[...], preferred_element_type=f32)
    k_ref[...] = kv[:, :dkv].astype(bf16)
    vt_ref[...] = kv[:, dkv:].T.astype(bf16)


def _attn_kernel(sink_ref, x_ref, qt_ref, k_ref, vt_ref, kc_ref, vct_ref, mod_ref, g_ref, wo_ref, o_ref):
    for sub in range(x_ref.shape[0] // Q_ROWS):
        _attn_tile(sub, sink_ref, x_ref, qt_ref, k_ref, vt_ref, kc_ref[...], vct_ref[...], mod_ref, g_ref, wo_ref,
                   o_ref)


def _attn_tile(sub, sink_ref, x_ref, qt_ref, k_ref, vt_ref, kc, vct, mod_ref, g_ref, wo_ref, o_ref):
    tq = Q_ROWS
    d = x_ref.shape[1]
    l = k_ref.shape[0]
    span = tq + 2 * WINDOW
    group = d // HEAD_DIM // N_KV_HEADS
    lanes = group * tq
    rows = slice(sub * tq, (sub + 1) * tq)
    start = pl.program_id(1) * x_ref.shape[0] + sub * tq
    ws = pl.multiple_of(jnp.clip(start - WINDOW, 0, l - span), LANES)
    kw = k_ref[pl.ds(ws, span), :]
    slab = ws // LANES
    vtw = jnp.concatenate([vt_ref[slab + i] for i in range(span // LANES)], axis=1)
    qt = qt_ref[:, rows]
    kj = lax.broadcasted_iota(jnp.int32, (span, lanes), 0)
    qi = lax.broadcasted_iota(jnp.int32, (span, lanes), 1) % tq
    valid = jnp.abs(kj - qi + (ws - start)) <= WINDOW

    def with_ones(v):
        row = lax.broadcasted_iota(jnp.int32, (2 * SUBLANES, v.shape[1]), 0)
        return jnp.concatenate([v, (row == 0).astype(bf16)], axis=0)

    def scores(kh):
        heads = range(kh * group, (kh + 1) * group)
        qg = jnp.concatenate([qt[h * HEAD_DIM:(h + 1) * HEAD_DIM] for h in heads], axis=1)
        pair, odd = divmod(kh, LANES // HEAD_DIM)
        zero = jnp.zeros_like(qg)
        qz = jnp.concatenate([zero, qg] if odd else [qg, zero], axis=0)
        kcols = slice(pair * LANES, (pair + 1) * LANES)
        return (jnp.dot(kw[:, kcols], qz, preferred_element_type=f32),
                jnp.dot(kc[:, kcols], qz, preferred_element_type=f32))

    def weights(kh, s_lat, s_ctx):
        heads = range(kh * group, (kh + 1) * group)
        s_lat = jnp.where(valid, s_lat, NEG_INF)
        sink = jnp.concatenate([jnp.full((1, tq), sink_ref[h] * LOG2E, f32) for h in heads], axis=1)
        m = jnp.maximum(jnp.maximum(jnp.max(s_lat, axis=0, keepdims=True),
                                    jnp.max(s_ctx, axis=0, keepdims=True)), sink)
        return jnp.exp2(s_lat - m).astype(bf16), jnp.exp2(s_ctx - m).astype(bf16), jnp.exp2(sink - m)

    def mix(kh, p_lat, p_ctx, p_sink):
        vrows = slice(kh * HEAD_DIM, (kh + 1) * HEAD_DIM)
        o = (jnp.dot(with_ones(vtw[vrows]), p_lat, preferred_element_type=f32)
             + jnp.dot(with_ones(vct[vrows]), p_ctx, preferred_element_type=f32))
        o = o[:HEAD_DIM] / (o[HEAD_DIM:HEAD_DIM + 1] + p_sink)
        return [o[:, i * tq:(i + 1) * tq] for i in range(group)]

    pieces = []
    s_next = scores(0)
    p_prev = None
    for kh in range(N_KV_HEADS):
        s_cur = s_next
        if kh + 1 < N_KV_HEADS:
            s_next = scores(kh + 1)
        if p_prev is not None:
            pieces += mix(kh - 1, *p_prev)
        p_prev = weights(kh, *s_cur)
    pieces += mix(N_KV_HEADS - 1, *p_prev)
    att = jnp.concatenate(pieces, axis=0).T.astype(bf16)
    y = jnp.dot(att, wo_ref[...], preferred_element_type=f32)
    o_ref[rows, :] = x_ref[rows, :] + _rms(y, mod_ref[5:6] * g_ref[4:5])


def _attention(x, xc, mx, mc, g, w_qkv, w_o, sink):
    b, l, d = x.shape
    c = xc.shape[1]
    dq = w_o.shape[0]
    dkv = (w_qkv.shape[1] - dq) // 2
    tt = min(MIX_ROWS, l)
    cos, sin = _rope_tables(l)
    wb = w_qkv.astype(bf16)
    qt, k, vt = pl.pallas_call(
        functools.partial(_qkv_kernel, dq=dq, dkv=dkv),
        grid=(b, l // tt),
        in_specs=[pl.BlockSpec((None, tt, d), lambda i, t: (i, t, 0)),
                  pl.BlockSpec((None, N_MOD, d), lambda i, t: (i, 0, 0)),
                  pl.BlockSpec(g.shape, lambda i, t: (0, 0)),
                  _resident(wb.shape),
                  pl.BlockSpec((tt, LANES), lambda i, t: (t, 0)),
                  pl.BlockSpec((tt, LANES), lambda i, t: (t, 0))],
        out_specs=[pl.BlockSpec((None, dq, tt), lambda i, t: (i, 0, t)),
                   pl.BlockSpec((None, tt, dkv), lambda i, t: (i, t, 0)),
                   pl.BlockSpec((None, tt // LANES, dkv, LANES), lambda i, t: (i, t, 0, 0))],
        out_shape=[jax.ShapeDtypeStruct((b, dq, l), bf16),
                   jax.ShapeDtypeStruct((b, l, dkv), bf16),
                   jax.ShapeDtypeStruct((b, l // LANES, dkv, LANES), bf16)],
        compiler_params=_cparams(2),
        name="attn_qkv",
    )(x, mx, g, wb, cos, sin)
    kc, vct = pl.pallas_call(
        functools.partial(_kv_kernel, dkv=dkv),
        grid=(b,),
        in_specs=[pl.BlockSpec((None, c, d), lambda i: (i, 0, 0)),
                  pl.BlockSpec((None, N_MOD, d), lambda i: (0, 0, 0)),
                  pl.BlockSpec(g.shape, lambda i: (0, 0)),
                  _resident((d, 2 * dkv))],
        out_specs=[pl.BlockSpec((None, c, dkv), lambda i: (i, 0, 0)),
                   pl.BlockSpec((None, dkv, c), lambda i: (i, 0, 0))],
        out_shape=[jax.ShapeDtypeStruct((b, c, dkv), bf16), jax.ShapeDtypeStruct((b, dkv, c), bf16)],
        compiler_params=_cparams(1),
        name="attn_ctx_kv",
    )(xc, mc, g, wb[:, dq:])
    tq = Q_TILES_PER_STEP * Q_ROWS
    assert l % tq == 0 and l >= Q_ROWS + 2 * WINDOW and Q_ROWS == LANES
    return pl.pallas_call(
        _attn_kernel,
        grid=(b, l // tq),
        in_specs=[pl.BlockSpec(memory_space=pltpu.SMEM),
                  pl.BlockSpec((None, tq, d), lambda i, n: (i, n, 0)),
                  pl.BlockSpec((None, dq, tq), lambda i, n: (i, 0, n)),
                  pl.BlockSpec((None, l, dkv), lambda i, n: (i, 0, 0)),
                  pl.BlockSpec((None, l // LANES, dkv, LANES), lambda i, n: (i, 0, 0, 0)),
                  pl.BlockSpec((None, c, dkv), lambda i, n: (i, 0, 0)),
                  pl.BlockSpec((None, dkv, c), lambda i, n: (i, 0, 0)),
                  pl.BlockSpec((None, N_MOD, d), lambda i, n: (i, 0, 0)),
                  pl.BlockSpec(g.shape, lambda i, n: (0, 0)),
                  _resident((dq, d))],
        out_specs=pl.BlockSpec((None, tq, d), lambda i, n: (i, n, 0)),
        out_shape=jax.ShapeDtypeStruct(x.shape, f32),
        compiler_params=_cparams(2),
        name="attn_core",
    )(sink, x, qt, k, vt, kc, vct, mx, g, w_o.astype(bf16))


def kernel(x, c, ctx, c_ctx, w_mod, b_mod, norm_g, ffn_w_in, ffn_w_out, rg_w_in, rg_conv_w, rg_conv_b,
           rg_gate_w, rg_gate_b, rg_lambda, rg_w_out, attn_w_qkv, attn_w_o, attn_sink):
    b, l, d = x.shape
    n_ctx = ctx.shape[1]
    depth = w_mod.shape[0]
    assert depth == 2, "layer 0 = RG-LRU with context output, layer 1 = windowed attention (last)"
    rows = b + 1
    rows_pad = -(-rows // 16) * 16
    cond = jnp.zeros((rows_pad, d), f32).at[:b].set(c).at[b].set(c_ctx)
    mods = _modulation(cond, w_mod, b_mod)
    xc = ctx
    for i in range(depth):
        mx, mc, g = mods[i, :b], mods[i, b:b + 1], norm_g[i]
        flat = lambda a: a.reshape(1, b * n_ctx, d)
        w1 = _ffn_weights(ffn_w_in[i, 0], ffn_w_out[i, 0])
        w2 = _ffn_weights(ffn_w_in[i, 1], ffn_w_out[i, 1])
        x = _ffn(x, mx, g, w1, 0)
        xc = _ffn(flat(xc), mc, g, w1, 0).reshape(b, n_ctx, d)
        if i == 0:
            x, xc = _rglru(x, xc, mx, mc, g, rg_w_in[0], rg_conv_w[0], rg_conv_b[0], rg_gate_w[0],
                           rg_gate_b[0], rg_lambda[0], rg_w_out[0])
            x = _ffn(x, mx, g, w2, 2)
            xc = _ffn(flat(xc), mc, g, w2, 2).reshape(b, n_ctx, d)
        else:
            x = _attention(x, xc, mx, mc, g, attn_w_qkv[0], attn_w_o[0], attn_sink[0])
            x = _ffn(x, mx, g, w2, 2)
    return x
```

```python
import functools

import jax
import jax.numpy as jnp
from jax import lax
from jax.experimental import pallas as pl
from jax.experimental.pallas import tpu as pltpu

f32 = jnp.float32
bf16 = jnp.bfloat16

N_MOD = 9
FFN_RES_WEIGHT = 0.5
NORM_EPS = 1e-6
NEG_INF = -1e30
RG_BLOCKS = 4
CONV_W = 4
LRU_C = 8.0
HEAD_DIM = 64
N_KV_HEADS = 4
WINDOW = 128
GRID_W = 64
ROPE_BASE = 10000.0
ROPE_AXIS_DIM = HEAD_DIM // 2
LOG2E = 1.4426950408889634
Q_SCALE = HEAD_DIM ** -0.5 * LOG2E

LANES = 128
SUBLANES = 8
VMEM_LIMIT = 56 * 1024 * 1024
FFN_CHUNK = 256
FFN_ROWS = 512
FFN_TILES_PER_STEP = 2
MIX_ROWS = 256
RG_TILES_PER_STEP = 2
QKV_ROWS = 512
Q_ROWS = 128
Q_TILES_PER_STEP = 2


def _cparams(n_axes):
    return pltpu.CompilerParams(dimension_semantics=("arbitrary",) * n_axes,
                                vmem_limit_bytes=VMEM_LIMIT)


def _resident(shape):
    return pl.BlockSpec(shape, lambda *_: (0,) * len(shape), pipeline_mode=pl.Buffered(1))


def _rms(x, gain):
    ms = jnp.mean(x * x, axis=-1, keepdims=True)
    return x * lax.rsqrt(ms + NORM_EPS) * gain


def _modulate(x, g, shift, scale):
    return _rms(x, g * (1.0 + scale)) + shift


def _mod_kernel(cond_ref, w_ref, b_ref, o_ref):
    c = cond_ref[...]
    s = (c * jax.nn.sigmoid(c)).astype(bf16)
    o_ref[...] = jnp.dot(s, w_ref[...].astype(bf16), preferred_element_type=f32) + b_ref[...]


def _modulation(cond, w_mod, b_mod):
    depth, d, n = w_mod.shape
    r = cond.shape[0]
    tn = 1536
    out = pl.pallas_call(
        _mod_kernel,
        grid=(depth, n // tn),
        in_specs=[pl.BlockSpec((r, d), lambda i, j: (0, 0)),
                  pl.BlockSpec((None, d, tn), lambda i, j: (i, 0, j)),
                  pl.BlockSpec((None, 1, tn), lambda i, j: (i, 0, j))],
        out_specs=pl.BlockSpec((None, r, tn), lambda i, j: (i, 0, j)),
        out_shape=jax.ShapeDtypeStruct((depth, r, n), f32),
        compiler_params=_cparams(2),
        name="modulation",
    )(cond, w_mod, b_mod.reshape(depth, 1, n))
    return out.reshape(depth, r, N_MOD, d)


def _ffn_kernel(x_ref, mod_ref, g_ref, win_ref, wout_ref, o_ref, *, k):
    f = wout_ref.shape[0]
    tf = FFN_CHUNK
    n_chunks = f // tf
    tm = min(FFN_ROWS, x_ref.shape[0])
    gain_in = g_ref[k:k + 1] * (1.0 + mod_ref[3 * k + 1:3 * k + 2])
    gain_out = FFN_RES_WEIGHT * mod_ref[3 * k + 2:3 * k + 3] * g_ref[3 + k:4 + k]
    subs = [slice(i * tm, (i + 1) * tm) for i in range(x_ref.shape[0] // tm)]
    hbs = [(_rms(x_ref[rows, :], gain_in) + mod_ref[3 * k:3 * k + 1]).astype(bf16) for rows in subs]
    for rows, hb in zip(subs, hbs):

        def hidden(j):
            gate = jnp.dot(hb, win_ref[:, j * tf:(j + 1) * tf], preferred_element_type=f32)
            up = jnp.dot(hb, win_ref[:, f + j * tf:f + (j + 1) * tf], preferred_element_type=f32)
            return (gate * jax.nn.sigmoid(gate) * up).astype(bf16)

        a_prev = hidden(0)
        y = None
        for j in range(1, n_chunks + 1):
            part = jnp.dot(a_prev, wout_ref[(j - 1) * tf:j * tf, :], preferred_element_type=f32)
            y = part if y is None else y + part
            if j < n_chunks:
                a_prev = hidden(j)
        o_ref[rows, :] = x_ref[rows, :] + _rms(y, gain_out)


def _ffn_weights(w_in, w_out):
    assert w_out.shape[0] % FFN_CHUNK == 0
    return w_in.astype(bf16), w_out.astype(bf16)


def _ffn(x, mod, g, weights, k):
    win, wout = weights
    bn, ln, d = x.shape
    tm = min(FFN_TILES_PER_STEP * FFN_ROWS, ln)
    assert ln % tm == 0 and tm % min(FFN_ROWS, ln) == 0
    return pl.pallas_call(
        functools.partial(_ffn_kernel, k=k),
        grid=(bn, ln // tm),
        in_specs=[pl.BlockSpec((None, tm, d), lambda b, t: (b, t, 0)),
                  pl.BlockSpec((None, N_MOD, d), lambda b, t: (b, 0, 0)),
                  pl.BlockSpec(g.shape, lambda b, t: (0, 0)),
                  _resident(win.shape),
                  _resident(wout.shape)],
        out_specs=pl.BlockSpec((None, tm, d), lambda b, t: (b, t, 0)),
        out_shape=jax.ShapeDtypeStruct(x.shape, f32),
        compiler_params=_cparams(2),
        name=f"ffn{k}",
    )(x, mod, g, win, wout)


def _segment_scan(a, b, c0, reverse):
    t, w = a.shape
    n = t // SUBLANES
    h_loc = [None] * n
    a_cum = [None] * n
    h = acc = None
    for k in (range(n - 1, -1, -1) if reverse else range(n)):
        ak, bk = a[k * SUBLANES:(k + 1) * SUBLANES], b[k * SUBLANES:(k + 1) * SUBLANES]
        h, acc = (bk, ak) if h is None else (ak * h + bk, ak * acc)
        h_loc[k], a_cum[k] = h, acc
    row = lax.broadcasted_iota(jnp.int32, (SUBLANES, w), 0)
    for j in (1, 2, 4):
        shift = SUBLANES - j if reverse else j
        m = (row < SUBLANES - j) if reverse else (row >= j)
        h_sh = pltpu.roll(h, shift, axis=0)
        a_sh = pltpu.roll(acc, shift, axis=0)
        h = jnp.where(m, acc * h_sh + h, h)
        acc = jnp.where(m, acc * a_sh, acc)
    end = acc * c0 + h
    entry = SUBLANES - 1 if reverse else 0
    c_in = jnp.where(row == entry, c0, pltpu.roll(end, SUBLANES - 1 if reverse else 1, axis=0))
    out = jnp.concatenate([h_loc[k] + a_cum[k] * c_in for k in range(n)], axis=0)
    last = 0 if reverse else SUBLANES - 1
    return out, end[last:last + 1]


def _rg_kernel(x_ref, c_ref, mx_ref, mc_ref, g_ref, perm_ref, permt_ref, win_ref, cw_ref, cb_ref, gw_ref,
               gb_ref, lam_ref, wout_ref, xo_ref, co_ref, xr_s, rx_s, gx_s, sf_s, h_s, *, nx):
    tt, d = c_ref.shape
    n_sub = x_ref.shape[0] // tt
    n_steps = nx // n_sub
    bw = d // RG_BLOCKS
    p = pl.program_id(1)
    s = pl.program_id(2)
    is_ctx = s == 0
    is_lat = s > 0
    g8 = SUBLANES
    front = 2 * g8
    back = g8

    def sub_rows(i):
        return slice(i * tt, (i + 1) * tt)

    def tile_base(tile):
        return tile * tt if isinstance(tile, int) else pl.multiple_of(tile * tt, tt)

    def select(flag, v):
        return (v if flag else jnp.zeros_like(v)) if isinstance(flag, bool) else jnp.where(flag, v, 0.0)

    def project(xt, mod_ref, tile):
        h = _modulate(xt, g_ref[1:2], mod_ref[3:4], mod_ref[4:5]).astype(bf16)
        hp = jnp.dot(perm_ref[...], h, preferred_element_type=f32).astype(bf16)
        gr = jnp.dot(hp, win_ref[...], preferred_element_type=f32)
        base = tile_base(tile)
        gx_s[pl.ds(base, tt), :] = jax.nn.gelu(gr[:, :d]).astype(bf16)
        xr_s[pl.ds(base + front, tt), :] = gr[:, d:]

    @pl.when(jnp.logical_and(p == 0, is_ctx))
    def _():
        xr_s[0:front, :] = jnp.zeros((front, d), f32)
        xr_s[front + (nx + 1) * tt:front + back + (nx + 1) * tt, :] = jnp.zeros((back, d), f32)
        project(c_ref[...], mc_ref, 0)

    @pl.when(jnp.logical_and(p == 0, is_lat))
    def _():
        for i in range(n_sub):
            project(x_ref[sub_rows(i), :], mx_ref, 1 + (s - 1) * n_sub + i)

    def coeffs(rx, direction, blk):
        ri = jnp.dot(rx.astype(bf16), gw_ref[direction, blk], preferred_element_type=f32)
        cols = slice(blk * bw, (blk + 1) * bw)
        r = 1.0 / (1.0 + jnp.exp2(ri[:, :bw] + gb_ref[2 * direction:2 * direction + 1, cols]))
        i = 1.0 / (1.0 + jnp.exp2(ri[:, bw:] + gb_ref[2 * direction + 1:2 * direction + 2, cols]))
        lam = lam_ref[direction:direction + 1, cols]
        a = jnp.exp2(r * (-LRU_C * LOG2E * jax.nn.softplus(-lam)))
        v = 1.0 - a * a
        root = jnp.where(v > 0.0, v * lax.rsqrt(v), 0.0)
        return a, root * (i * rx)

    def forward(tile, from_zero, has_prev, has_next):
        base = tile_base(tile)
        row = lax.broadcasted_iota(jnp.int32, (g8, bw), 0)
        for blk in range(RG_BLOCKS):
            cols = slice(blk * bw, (blk + 1) * bw)
            cur = xr_s[pl.ds(base + front, tt), cols]
            prev2 = select(has_prev, xr_s[pl.ds(base + front - 2 * g8, g8), cols])
            prev1 = select(has_prev, xr_s[pl.ds(base + front - g8, g8), cols])
            nxt = select(has_next, xr_s[pl.ds(base + front + tt, g8), cols])
            m1 = jnp.where(row == 0, pltpu.roll(prev1, 1, axis=0), pltpu.roll(cur[tt - g8:], 1, axis=0))
            m2 = jnp.where(row == 0, pltpu.roll(prev2, 1, axis=0),
                           pltpu.roll(cur[tt - 2 * g8:tt - g8], 1, axis=0))
            p1 = jnp.where(row == g8 - 1, pltpu.roll(nxt, g8 - 1, axis=0), pltpu.roll(cur[:g8], g8 - 1, axis=0))
            rx = (cb_ref[0:1, cols]
                  + cw_ref[0:1, cols] * jnp.concatenate([m2, m1, cur[:tt - 2 * g8]], axis=0)
                  + cw_ref[1:2, cols] * jnp.concatenate([m1, cur[:tt - g8]], axis=0)
                  + cw_ref[2:3, cols] * cur
                  + cw_ref[3:4, cols] * jnp.concatenate([cur[g8:], p1], axis=0))
            rx_s[pl.ds(base, tt), cols] = rx
            a, b = coeffs(rx, 0, blk)
            h0 = jnp.zeros((1, bw), f32) if from_zero else h_s[0:1, cols]
            hf, h_out = _segment_scan(a, b, h0, reverse=False)
            sf_s[pl.ds(base, tt), cols] = hf.astype(bf16)
            h_s[0:1, cols] = h_out

    @pl.when(jnp.logical_and(p == 1, is_ctx))
    def _():
        forward(0, True, False, False)

    @pl.when(jnp.logical_and(p == 1, is_lat))
    def _():
        for i in range(n_sub):
            forward(1 + (s - 1) * n_sub + i, False,
                    True if i > 0 else s >= 2, True if i < n_sub - 1 else s < n_steps)

    def backward(tile, from_zero, mod_ref):
        base = tile_base(tile)
        zs = []
        for blk in range(RG_BLOCKS):
            cols = slice(blk * bw, (blk + 1) * bw)
            rx = rx_s[pl.ds(base, tt), cols]
            a, b = coeffs(rx, 1, blk)
            h0 = jnp.zeros((1, bw), f32) if from_zero else h_s[1:2, cols]
            hb, h_out = _segment_scan(a, b, h0, reverse=True)
            h_s[1:2, cols] = h_out
            tot = sf_s[pl.ds(base, tt), cols].astype(f32) + hb
            zs.append((gx_s[pl.ds(base, tt), cols].astype(f32) * tot).astype(bf16))
        z = jnp.dot(permt_ref[...], jnp.concatenate(zs, axis=1), preferred_element_type=f32)
        y = jnp.dot(z.astype(bf16), wout_ref[...], preferred_element_type=f32)
        return _rms(y, mod_ref[5:6] * g_ref[4:5])

    @pl.when(jnp.logical_and(p == 2, is_ctx))
    def _():
        co_ref[...] = c_ref[...] + backward(0, True, mc_ref)

    @pl.when(jnp.logical_and(p == 2, is_lat))
    def _():
        for i in reversed(range(n_sub)):
            upd = backward(1 + (n_steps - s) * n_sub + i, False, mx_ref)
            xo_ref[sub_rows(i), :] = x_ref[sub_rows(i), :] + upd


def _rglru(x, xc, mx, mc, g, w_in, conv_w, conv_b, gate_w, gate_b, lam, w_out):
    b, l, d = x.shape
    c = xc.shape[1]
    tt = MIX_ROWS
    n_sub = RG_TILES_PER_STEP
    assert c == tt and l % (n_sub * tt) == 0, "context must be exactly one mixer tile"
    nx = l // tt
    n_steps = nx // n_sub
    gw = (-LOG2E * jnp.concatenate([gate_w[:, 0], gate_w[:, 1]], axis=-1)).astype(bf16)
    gb = -LOG2E * gate_b.reshape(4, d)
    rows = (nx + 1) * tt
    dst = jnp.arange(tt)
    src = (dst % SUBLANES) * (tt // SUBLANES) + dst // SUBLANES
    perm = (src[:, None] == jnp.arange(tt)[None, :]).astype(bf16)

    def x_tile(p, s):
        return jnp.where(p == 0, jnp.maximum(s - 1, 0),
                         jnp.where(p == 1, n_steps - 1, n_steps - jnp.maximum(s, 1)))

    def xo_tile(p, s):
        return jnp.where(p == 2, n_steps - jnp.maximum(s, 1), n_steps - 1)

    return pl.pallas_call(
        functools.partial(_rg_kernel, nx=nx),
        grid=(b, 3, n_steps + 1),
        in_specs=[pl.BlockSpec((None, n_sub * tt, d), lambda i, p, s: (i, x_tile(p, s), 0)),
                  pl.BlockSpec((None, tt, d), lambda i, p, s: (i, 0, 0)),
                  pl.BlockSpec((None, N_MOD, d), lambda i, p, s: (i, 0, 0)),
                  pl.BlockSpec((None, N_MOD, d), lambda i, p, s: (0, 0, 0)),
                  pl.BlockSpec(g.shape, lambda i, p, s: (0, 0)),
                  _resident((tt, tt)),
                  _resident((tt, tt)),
                  _resident((d, 2 * d)),
                  pl.BlockSpec(conv_w.shape, lambda i, p, s: (0, 0)),
                  pl.BlockSpec((1, d), lambda i, p, s: (0, 0)),
                  _resident(gw.shape),
                  pl.BlockSpec(gb.shape, lambda i, p, s: (0, 0)),
                  pl.BlockSpec(lam.shape, lambda i, p, s: (0, 0)),
                  _resident((d, d))],
        out_specs=[pl.BlockSpec((None, n_sub * tt, d), lambda i, p, s: (i, xo_tile(p, s), 0)),
                   pl.BlockSpec((None, tt, d), lambda i, p, s: (i, 0, 0))],
        out_shape=[jax.ShapeDtypeStruct(x.shape, f32), jax.ShapeDtypeStruct(xc.shape, f32)],
        scratch_shapes=[pltpu.VMEM((rows + 3 * SUBLANES, d), f32),
                        pltpu.VMEM((rows, d), f32),
                        pltpu.VMEM((rows, d), bf16),
                        pltpu.VMEM((rows, d), bf16),
                        pltpu.VMEM((SUBLANES, d), f32)],
        compiler_params=_cparams(3),
        name="rglru",
    )(x, xc, mx, mc, g, perm, perm.T, w_in.astype(bf16), conv_w, conv_b.reshape(1, d), gw, gb, lam,
      w_out.astype(bf16))


def _rope_tables(l):
    rows = l // GRID_W
    row = jnp.repeat(jnp.arange(rows, dtype=f32), GRID_W)
    col = jnp.tile(jnp.arange(GRID_W, dtype=f32), rows)
    inv = 1.0 / (ROPE_BASE ** (jnp.arange(0, ROPE_AXIS_DIM, 2, dtype=f32) / ROPE_AXIS_DIM))
    ar, ac = row[:, None] * inv, col[:, None] * inv
    cos = jnp.concatenate([jnp.cos(ar), jnp.cos(ar), jnp.cos(ac), jnp.cos(ac)], axis=-1)
    sin = jnp.concatenate([-jnp.sin(ar), jnp.sin(ar), -jnp.sin(ac), jnp.sin(ac)], axis=-1)
    reps = LANES // HEAD_DIM
    return jnp.tile(cos, (1, reps)), jnp.tile(sin, (1, reps))


def _qkv_kernel(x_ref, mod_ref, g_ref, w_ref, cos_ref, sin_ref, qt_ref, k_ref, vt_ref, *, dq, dkv):
    tt = x_ref.shape[0]
    h = _modulate(x_ref[...], g_ref[1:2], mod_ref[3:4], mod_ref[4:5])
    qkv = jnp.dot(h.astype(bf16), w_ref[...], preferred_element_type=f32)
    cos = cos_ref[...]
    sin = sin_ref[...]
    lane = lax.broadcasted_iota(jnp.int32, cos.shape, 1)
    half = ROPE_AXIS_DIM // 2
    first = (lane % ROPE_AXIS_DIM) < half

    def rope(z):
        partner = jnp.where(first, pltpu.roll(z, LANES - half, axis=1), pltpu.roll(z, half, axis=1))
        return z * cos + partner * sin

    for j in range(dq // LANES):
        cols = slice(j * LANES, (j + 1) * LANES)
        qt_ref[cols, :] = (rope(qkv[:, cols]) * Q_SCALE).T.astype(bf16)
    for j in range(dkv // LANES):
        k_ref[:, j * LANES:(j + 1) * LANES] = rope(qkv[:, dq + j * LANES:dq + (j + 1) * LANES]).astype(bf16)
    vt = qkv[:, dq + dkv:].T.astype(bf16)
    for j in range(tt // LANES):
        vt_ref[j] = vt[:, j * LANES:(j + 1) * LANES]


def _kv_kernel(x_ref, mod_ref, g_ref, w_ref, k_ref, vt_ref, *, dkv):
    h = _modulate(x_ref[...], g_ref[1:2], mod_ref[3:4], mod_ref[4:5])
    kv = jnp.dot(h.astype(bf16), w_ref[...], preferred_element_type=f32)
    k_ref[...] = kv[:, :dkv].astype(bf16)
    vt_ref[...] = kv[:, dkv:].T.astype(bf16)


def _attn_kernel(sink_ref, x_ref, qt_ref, k_ref, vt_ref, kc_ref, vct_ref, mod_ref, g_ref, wo_ref, o_ref):
    for sub in range(x_ref.shape[0] // Q_ROWS):
        _attn_tile(sub, sink_ref, x_ref, qt_ref, k_ref, vt_ref, kc_ref[...], vct_ref[...], mod_ref, g_ref, wo_ref,
                   o_ref)


def _attn_tile(sub, sink_ref, x_ref, qt_ref, k_ref, vt_ref, kc, vct, mod_ref, g_ref, wo_ref, o_ref):
    tq = Q_ROWS
    d = x_ref.shape[1]
    l = k_ref.shape[0]
    span = tq + 2 * WINDOW
    group = d // HEAD_DIM // N_KV_HEADS
    lanes = group * tq
    rows = slice(sub * tq, (sub + 1) * tq)
    start = pl.program_id(1) * x_ref.shape[0] + sub * tq
    ws = pl.multiple_of(jnp.clip(start - WINDOW, 0, l - span), LANES)
    kw = k_ref[pl.ds(ws, span), :]
    slab = ws // LANES
    vtw = jnp.concatenate([vt_ref[slab + i] for i in range(span // LANES)], axis=1)
    qt = qt_ref[:, rows]
    kj = lax.broadcasted_iota(jnp.int32, (span, lanes), 0)
    qi = lax.broadcasted_iota(jnp.int32, (span, lanes), 1) % tq
    valid = jnp.abs(kj - qi + (ws - start)) <= WINDOW

    def with_ones(v):
        row = lax.broadcasted_iota(jnp.int32, (2 * SUBLANES, v.shape[1]), 0)
        return jnp.concatenate([v, (row == 0).astype(bf16)], axis=0)

    def scores(kh):
        heads = range(kh * group, (kh + 1) * group)
        qg = jnp.concatenate([qt[h * HEAD_DIM:(h + 1) * HEAD_DIM] for h in heads], axis=1)
        pair, odd = divmod(kh, LANES // HEAD_DIM)
        zero = jnp.zeros_like(qg)
        qz = jnp.concatenate([zero, qg] if odd else [qg, zero], axis=0)
        kcols = slice(pair * LANES, (pair + 1) * LANES)
        return (jnp.dot(kw[:, kcols], qz, preferred_element_type=f32),
                jnp.dot(kc[:, kcols], qz, preferred_element_type=f32))

    def weights(kh, s_lat, s_ctx):
        heads = range(kh * group, (kh + 1) * group)
        s_lat = jnp.where(valid, s_lat, NEG_INF)
        sink = jnp.concatenate([jnp.full((1, tq), sink_ref[h] * LOG2E, f32) for h in heads], axis=1)
        m = jnp.maximum(jnp.maximum(jnp.max(s_lat, axis=0, keepdims=True),
                                    jnp.max(s_ctx, axis=0, keepdims=True)), sink)
        return jnp.exp2(s_lat - m).astype(bf16), jnp.exp2(s_ctx - m).astype(bf16), jnp.exp2(sink - m)

    def mix(kh, p_lat, p_ctx, p_sink):
        vrows = slice(kh * HEAD_DIM, (kh + 1) * HEAD_DIM)
        o = (jnp.dot(with_ones(vtw[vrows]), p_lat, preferred_element_type=f32)
             + jnp.dot(with_ones(vct[vrows]), p_ctx, preferred_element_type=f32))
        o = o[:HEAD_DIM] / (o[HEAD_DIM:HEAD_DIM + 1] + p_sink)
        return [o[:, i * tq:(i + 1) * tq] for i in range(group)]

    pieces = []
    s_next = scores(0)
    p_prev = None
    for kh in range(N_KV_HEADS):
        s_cur = s_next
        if kh + 1 < N_KV_HEADS:
            s_next = scores(kh + 1)
        if p_prev is not None:
            pieces += mix(kh - 1, *p_prev)
        p_prev = weights(kh, *s_cur)
    pieces += mix(N_KV_HEADS - 1, *p_prev)
    att = jnp.concatenate(pieces, axis=0).T.astype(bf16)
    y = jnp.dot(att, wo_ref[...], preferred_element_type=f32)
    o_ref[rows, :] = x_ref[rows, :] + _rms(y, mod_ref[5:6] * g_ref[4:5])


def _attention(x, xc, mx, mc, g, w_qkv, w_o, sink):
    b, l, d = x.shape
    c = xc.shape[1]
    dq = w_o.shape[0]
    dkv = (w_qkv.shape[1] - dq) // 2
    tt = min(QKV_ROWS, l)
    assert l % tt == 0
    cos, sin = _rope_tables(l)
    wb = w_qkv.astype(bf16)
    qt, k, vt = pl.pallas_call(
        functools.partial(_qkv_kernel, dq=dq, dkv=dkv),
        grid=(b, l // tt),
        in_specs=[pl.BlockSpec((None, tt, d), lambda i, t: (i, t, 0)),
                  pl.BlockSpec((None, N_MOD, d), lambda i, t: (i, 0, 0)),
                  pl.BlockSpec(g.shape, lambda i, t: (0, 0)),
                  _resident(wb.shape),
                  pl.BlockSpec((tt, LANES), lambda i, t: (t, 0)),
                  pl.BlockSpec((tt, LANES), lambda i, t: (t, 0))],
        out_specs=[pl.BlockSpec((None, dq, tt), lambda i, t: (i, 0, t)),
                   pl.BlockSpec((None, tt, dkv), lambda i, t: (i, t, 0)),
                   pl.BlockSpec((None, tt // LANES, dkv, LANES), lambda i, t: (i, t, 0, 0))],
        out_shape=[jax.ShapeDtypeStruct((b, dq, l), bf16),
                   jax.ShapeDtypeStruct((b, l, dkv), bf16),
                   jax.ShapeDtypeStruct((b, l // LANES, dkv, LANES), bf16)],
        compiler_params=_cparams(2),
        name="attn_qkv",
    )(x, mx, g, wb, cos, sin)
    kc, vct = pl.pallas_call(
        functools.partial(_kv_kernel, dkv=dkv),
        grid=(b,),
        in_specs=[pl.BlockSpec((None, c, d), lambda i: (i, 0, 0)),
                  pl.BlockSpec((None, N_MOD, d), lambda i: (0, 0, 0)),
                  pl.BlockSpec(g.shape, lambda i: (0, 0)),
                  _resident((d, 2 * dkv))],
        out_specs=[pl.BlockSpec((None, c, dkv), lambda i: (i, 0, 0)),
                   pl.BlockSpec((None, dkv, c), lambda i: (i, 0, 0))],
        out_shape=[jax.ShapeDtypeStruct((b, c, dkv), bf16), jax.ShapeDtypeStruct((b, dkv, c), bf16)],
        compiler_params=_cparams(1),
        name="attn_ctx_kv",
    )(xc, mc, g, wb[:, dq:])
    tq = Q_TILES_PER_STEP * Q_ROWS
    assert l % tq == 0 and l >= Q_ROWS + 2 * WINDOW and Q_ROWS == LANES
    return pl.pallas_call(
        _attn_kernel,
        grid=(b, l // tq),
        in_specs=[pl.BlockSpec(memory_space=pltpu.SMEM),
                  pl.BlockSpec((None, tq, d), lambda i, n: (i, n, 0)),
                  pl.BlockSpec((None, dq, tq), lambda i, n: (i, 0, n)),
                  pl.BlockSpec((None, l, dkv), lambda i, n: (i, 0, 0)),
                  pl.BlockSpec((None, l // LANES, dkv, LANES), lambda i, n: (i, 0, 0, 0)),
                  pl.BlockSpec((None, c, dkv), lambda i, n: (i, 0, 0)),
                  pl.BlockSpec((None, dkv, c), lambda i, n: (i, 0, 0)),
                  pl.BlockSpec((None, N_MOD, d), lambda i, n: (i, 0, 0)),
                  pl.BlockSpec(g.shape, lambda i, n: (0, 0)),
                  _resident((dq, d))],
        out_specs=pl.BlockSpec((None, tq, d), lambda i, n: (i, n, 0)),
        out_shape=jax.ShapeDtypeStruct(x.shape, f32),
        compiler_params=_cparams(2),
        name="attn_core",
    )(sink, x, qt, k, vt, kc, vct, mx, g, w_o.astype(bf16))


def kernel(x, c, ctx, c_ctx, w_mod, b_mod, norm_g, ffn_w_in, ffn_w_out, rg_w_in, rg_conv_w, rg_conv_b,
           rg_gate_w, rg_gate_b, rg_lambda, rg_w_out, attn_w_qkv, attn_w_o, attn_sink):
    b, l, d = x.shape
    n_ctx = ctx.shape[1]
    depth = w_mod.shape[0]
    assert depth == 2, "layer 0 = RG-LRU with context output, layer 1 = windowed attention (last)"
    rows = b + 1
    rows_pad = -(-rows // 16) * 16
    cond = jnp.zeros((rows_pad, d), f32).at[:b].set(c).at[b].set(c_ctx)
    mods = _modulation(cond, w_mod, b_mod)
    xc = ctx
    for i in range(depth):
        mx, mc, g = mods[i, :b], mods[i, b:b + 1], norm_g[i]
        flat = lambda a: a.reshape(1, b * n_ctx, d)
        w1 = _ffn_weights(ffn_w_in[i, 0], ffn_w_out[i, 0])
        w2 = _ffn_weights(ffn_w_in[i, 1], ffn_w_out[i, 1])
        x = _ffn(x, mx, g, w1, 0)
        xc = _ffn(flat(xc), mc, g, w1, 0).reshape(b, n_ctx, d)
        if i == 0:
            x, xc = _rglru(x, xc, mx, mc, g, rg_w_in[0], rg_conv_w[0], rg_conv_b[0], rg_gate_w[0],
                           rg_gate_b[0], rg_lambda[0], rg_w_out[0])
            x = _ffn(x, mx, g, w2, 2)
            xc = _ffn(flat(xc), mc, g, w2, 2).reshape(b, n_ctx, d)
        else:
            x = _attention(x, xc, mx, mc, g, attn_w_qkv[0], attn_w_o[0], attn_sink[0])
            x = _ffn(x, mx, g, w2, 2)
    return x
```

```python
import functools

import jax
import jax.numpy as jnp
from jax import lax
from jax.experimental import pallas as pl
from jax.experimental.pallas import tpu as pltpu

f32 = jnp.float32
bf16 = jnp.bfloat16

N_MOD = 9
FFN_RES_WEIGHT = 0.5
NORM_EPS = 1e-6
NEG_INF = -1e30
RG_BLOCKS = 4
CONV_W = 4
LRU_C = 8.0
HEAD_DIM = 64
N_KV_HEADS = 4
WINDOW = 128
GRID_W = 64
ROPE_BASE = 10000.0
ROPE_AXIS_DIM = HEAD_DIM // 2
LOG2E = 1.4426950408889634
Q_SCALE = HEAD_DIM ** -0.5 * LOG2E

LANES = 128
SUBLANES = 8
VMEM_LIMIT = 56 * 1024 * 1024
FFN_CHUNK = 256
FFN_ROWS = 512
FFN_TILES_PER_STEP = 2
MIX_ROWS = 256
RG_TILES_PER_STEP = 2
QKV_ROWS = 512
Q_ROWS = 128
Q_TILES_PER_STEP = 4


def _cparams(n_axes):
    return pltpu.CompilerParams(dimension_semantics=("arbitrary",) * n_axes,
                                vmem_limit_bytes=VMEM_LIMIT)


def _resident(shape):
    return pl.BlockSpec(shape, lambda *_: (0,) * len(shape), pipeline_mode=pl.Buffered(1))


def _rms(x, gain):
    ms = jnp.mean(x * x, axis=-1, keepdims=True)
    return x * lax.rsqrt(ms + NORM_EPS) * gain


def _modulate(x, g, shift, scale):
    return _rms(x, g * (1.0 + scale)) + shift


def _mod_kernel(cond_ref, w_ref, b_ref, o_ref):
    c = cond_ref[...]
    s = (c * jax.nn.sigmoid(c)).astype(bf16)
    o_ref[...] = jnp.dot(s, w_ref[...].astype(bf16), preferred_element_type=f32) + b_ref[...]


def _modulation(cond, w_mod, b_mod):
    depth, d, n = w_mod.shape
    r = cond.shape[0]
    tn = 1536
    out = pl.pallas_call(
        _mod_kernel,
        grid=(depth, n // tn),
        in_specs=[pl.BlockSpec((r, d), lambda i, j: (0, 0)),
                  pl.BlockSpec((None, d, tn), lambda i, j: (i, 0, j)),
                  pl.BlockSpec((None, 1, tn), lambda i, j: (i, 0, j))],
        out_specs=pl.BlockSpec((None, r, tn), lambda i, j: (i, 0, j)),
        out_shape=jax.ShapeDtypeStruct((depth, r, n), f32),
        compiler_params=_cparams(2),
        name="modulation",
    )(cond, w_mod, b_mod.reshape(depth, 1, n))
    return out.reshape(depth, r, N_MOD, d)


def _ffn_kernel(x_ref, mod_ref, g_ref, win_ref, wout_ref, o_ref, *, k):
    f = wout_ref.shape[0]
    tf = FFN_CHUNK
    n_chunks = f // tf
    tm = min(FFN_ROWS, x_ref.shape[0])
    gain_in = g_ref[k:k + 1] * (1.0 + mod_ref[3 * k + 1:3 * k + 2])
    gain_out = FFN_RES_WEIGHT * mod_ref[3 * k + 2:3 * k + 3] * g_ref[3 + k:4 + k]
    subs = [slice(i * tm, (i + 1) * tm) for i in range(x_ref.shape[0] // tm)]
    hbs = [(_rms(x_ref[rows, :], gain_in) + mod_ref[3 * k:3 * k + 1]).astype(bf16) for rows in subs]
    for rows, hb in zip(subs, hbs):

        def hidden(j):
            gate = jnp.dot(hb, win_ref[:, j * tf:(j + 1) * tf], preferred_element_type=f32)
            up = jnp.dot(hb, win_ref[:, f + j * tf:f + (j + 1) * tf], preferred_element_type=f32)
            return (gate * jax.nn.sigmoid(gate) * up).astype(bf16)

        a_prev = hidden(0)
        y = None
        for j in range(1, n_chunks + 1):
            part = jnp.dot(a_prev, wout_ref[(j - 1) * tf:j * tf, :], preferred_element_type=f32)
            y = part if y is None else y + part
            if j < n_chunks:
                a_prev = hidden(j)
        o_ref[rows, :] = x_ref[rows, :] + _rms(y, gain_out)


def _ffn_weights(w_in, w_out):
    assert w_out.shape[0] % FFN_CHUNK == 0
    return w_in.astype(bf16), w_out.astype(bf16)


def _ffn(x, mod, g, weights, k):
    win, wout = weights
    bn, ln, d = x.shape
    tm = min(FFN_TILES_PER_STEP * FFN_ROWS, ln)
    assert ln % tm == 0 and tm % min(FFN_ROWS, ln) == 0
    return pl.pallas_call(
        functools.partial(_ffn_kernel, k=k),
        grid=(bn, ln // tm),
        in_specs=[pl.BlockSpec((None, tm, d), lambda b, t: (b, t, 0)),
                  pl.BlockSpec((None, N_MOD, d), lambda b, t: (b, 0, 0)),
                  pl.BlockSpec(g.shape, lambda b, t: (0, 0)),
                  _resident(win.shape),
                  _resident(wout.shape)],
        out_specs=pl.BlockSpec((None, tm, d), lambda b, t: (b, t, 0)),
        out_shape=jax.ShapeDtypeStruct(x.shape, f32),
        compiler_params=_cparams(2),
        name=f"ffn{k}",
    )(x, mod, g, win, wout)


def _segment_scan(a, b, c0, reverse):
    t, w = a.shape
    n = t // SUBLANES
    h_loc = [None] * n
    a_cum = [None] * n
    h = acc = None
    for k in (range(n - 1, -1, -1) if reverse else range(n)):
        ak, bk = a[k * SUBLANES:(k + 1) * SUBLANES], b[k * SUBLANES:(k + 1) * SUBLANES]
        h, acc = (bk, ak) if h is None else (ak * h + bk, ak * acc)
        h_loc[k], a_cum[k] = h, acc
    row = lax.broadcasted_iota(jnp.int32, (SUBLANES, w), 0)
    for j in (1, 2, 4):
        shift = SUBLANES - j if reverse else j
        m = (row < SUBLANES - j) if reverse else (row >= j)
        h_sh = pltpu.roll(h, shift, axis=0)
        a_sh = pltpu.roll(acc, shift, axis=0)
        h = jnp.where(m, acc * h_sh + h, h)
        acc = jnp.where(m, acc * a_sh, acc)
    end = acc * c0 + h
    entry = SUBLANES - 1 if reverse else 0
    c_in = jnp.where(row == entry, c0, pltpu.roll(end, SUBLANES - 1 if reverse else 1, axis=0))
    out = jnp.concatenate([h_loc[k] + a_cum[k] * c_in for k in range(n)], axis=0)
    last = 0 if reverse else SUBLANES - 1
    return out, end[last:last + 1]


def _rg_kernel(x_ref, c_ref, mx_ref, mc_ref, g_ref, perm_ref, permt_ref, win_ref, cw_ref, cb_ref, gw_ref,
               gb_ref, lam_ref, wout_ref, xo_ref, co_ref, xr_s, rx_s, gx_s, sf_s, h_s, *, nx):
    tt, d = c_ref.shape
    n_sub = x_ref.shape[0] // tt
    n_steps = nx // n_sub
    bw = d // RG_BLOCKS
    p = pl.program_id(1)
    s = pl.program_id(2)
    is_ctx = s == 0
    is_lat = s > 0
    g8 = SUBLANES
    front = 2 * g8
    back = g8

    def sub_rows(i):
        return slice(i * tt, (i + 1) * tt)

    def tile_base(tile):
        return tile * tt if isinstance(tile, int) else pl.multiple_of(tile * tt, tt)

    def select(flag, v):
        return (v if flag else jnp.zeros_like(v)) if isinstance(flag, bool) else jnp.where(flag, v, 0.0)

    def project(xt, mod_ref, tile):
        h = _modulate(xt, g_ref[1:2], mod_ref[3:4], mod_ref[4:5]).astype(bf16)
        hp = jnp.dot(perm_ref[...], h, preferred_element_type=f32).astype(bf16)
        gr = jnp.dot(hp, win_ref[...], preferred_element_type=f32)
        base = tile_base(tile)
        gx_s[pl.ds(base, tt), :] = jax.nn.gelu(gr[:, :d]).astype(bf16)
        xr_s[pl.ds(base + front, tt), :] = gr[:, d:]

    @pl.when(jnp.logical_and(p == 0, is_ctx))
    def _():
        xr_s[0:front, :] = jnp.zeros((front, d), f32)
        xr_s[front + (nx + 1) * tt:front + back + (nx + 1) * tt, :] = jnp.zeros((back, d), f32)
        project(c_ref[...], mc_ref, 0)

    @pl.when(jnp.logical_and(p == 0, is_lat))
    def _():
        for i in range(n_sub):
            project(x_ref[sub_rows(i), :], mx_ref, 1 + (s - 1) * n_sub + i)

    def coeffs(rx, direction, blk):
        ri = jnp.dot(rx.astype(bf16), gw_ref[direction, blk], preferred_element_type=f32)
        cols = slice(blk * bw, (blk + 1) * bw)
        r = 1.0 / (1.0 + jnp.exp2(ri[:, :bw] + gb_ref[2 * direction:2 * direction + 1, cols]))
        i = 1.0 / (1.0 + jnp.exp2(ri[:, bw:] + gb_ref[2 * direction + 1:2 * direction + 2, cols]))
        lam = lam_ref[direction:direction + 1, cols]
        a = jnp.exp2(r * (-LRU_C * LOG2E * jax.nn.softplus(-lam)))
        v = 1.0 - a * a
        root = jnp.where(v > 0.0, v * lax.rsqrt(v), 0.0)
        return a, root * (i * rx)

    def forward(tile, from_zero, has_prev, has_next):
        base = tile_base(tile)
        row = lax.broadcasted_iota(jnp.int32, (g8, bw), 0)
        for blk in range(RG_BLOCKS):
            cols = slice(blk * bw, (blk + 1) * bw)
            cur = xr_s[pl.ds(base + front, tt), cols]
            prev2 = select(has_prev, xr_s[pl.ds(base + front - 2 * g8, g8), cols])
            prev1 = select(has_prev, xr_s[pl.ds(base + front - g8, g8), cols])
            nxt = select(has_next, xr_s[pl.ds(base + front + tt, g8), cols])
            m1 = jnp.where(row == 0, pltpu.roll(prev1, 1, axis=0), pltpu.roll(cur[tt - g8:], 1, axis=0))
            m2 = jnp.where(row == 0, pltpu.roll(prev2, 1, axis=0),
                           pltpu.roll(cur[tt - 2 * g8:tt - g8], 1, axis=0))
            p1 = jnp.where(row == g8 - 1, pltpu.roll(nxt, g8 - 1, axis=0), pltpu.roll(cur[:g8], g8 - 1, axis=0))
            rx = (cb_ref[0:1, cols]
                  + cw_ref[0:1, cols] * jnp.concatenate([m2, m1, cur[:tt - 2 * g8]], axis=0)
                  + cw_ref[1:2, cols] * jnp.concatenate([m1, cur[:tt - g8]], axis=0)
                  + cw_ref[2:3, cols] * cur
                  + cw_ref[3:4, cols] * jnp.concatenate([cur[g8:], p1], axis=0))
            rx_s[pl.ds(base, tt), cols] = rx
            a, b = coeffs(rx, 0, blk)
            h0 = jnp.zeros((1, bw), f32) if from_zero else h_s[0:1, cols]
            hf, h_out = _segment_scan(a, b, h0, reverse=False)
            sf_s[pl.ds(base, tt), cols] = hf.astype(bf16)
            h_s[0:1, cols] = h_out

    @pl.when(jnp.logical_and(p == 1, is_ctx))
    def _():
        forward(0, True, False, False)

    @pl.when(jnp.logical_and(p == 1, is_lat))
    def _():
        for i in range(n_sub):
            forward(1 + (s - 1) * n_sub + i, False,
                    True if i > 0 else s >= 2, True if i < n_sub - 1 else s < n_steps)

    def backward(tile, from_zero, mod_ref):
        base = tile_base(tile)
        zs = []
        for blk in range(RG_BLOCKS):
            cols = slice(blk * bw, (blk + 1) * bw)
            rx = rx_s[pl.ds(base, tt), cols]
            a, b = coeffs(rx, 1, blk)
            h0 = jnp.zeros((1, bw), f32) if from_zero else h_s[1:2, cols]
            hb, h_out = _segment_scan(a, b, h0, reverse=True)
            h_s[1:2, cols] = h_out
            tot = sf_s[pl.ds(base, tt), cols].astype(f32) + hb
            zs.append((gx_s[pl.ds(base, tt), cols].astype(f32) * tot).astype(bf16))
        z = jnp.dot(permt_ref[...], jnp.concatenate(zs, axis=1), preferred_element_type=f32)
        y = jnp.dot(z.astype(bf16), wout_ref[...], preferred_element_type=f32)
        return _rms(y, mod_ref[5:6] * g_ref[4:5])

    @pl.when(jnp.logical_and(p == 2, is_ctx))
    def _():
        co_ref[...] = c_ref[...] + backward(0, True, mc_ref)

    @pl.when(jnp.logical_and(p == 2, is_lat))
    def _():
        for i in reversed(range(n_sub)):
            upd = backward(1 + (n_steps - s) * n_sub + i, False, mx_ref)
            xo_ref[sub_rows(i), :] = x_ref[sub_rows(i), :] + upd


def _rglru(x, xc, mx, mc, g, w_in, conv_w, conv_b, gate_w, gate_b, lam, w_out):
    b, l, d = x.shape
    c = xc.shape[1]
    tt = MIX_ROWS
    n_sub = RG_TILES_PER_STEP
    assert c == tt and l % (n_sub * tt) == 0, "context must be exactly one mixer tile"
    nx = l // tt
    n_steps = nx // n_sub
    gw = (-LOG2E * jnp.concatenate([gate_w[:, 0], gate_w[:, 1]], axis=-1)).astype(bf16)
    gb = -LOG2E * gate_b.reshape(4, d)
    rows = (nx + 1) * tt
    dst = jnp.arange(tt)
    src = (dst % SUBLANES) * (tt // SUBLANES) + dst // SUBLANES
    perm = (src[:, None] == jnp.arange(tt)[None, :]).astype(bf16)

    def x_tile(p, s):
        return jnp.where(p == 0, jnp.maximum(s - 1, 0),
                         jnp.where(p == 1, n_steps - 1, n_steps - jnp.maximum(s, 1)))

    def xo_tile(p, s):
        return jnp.where(p == 2, n_steps - jnp.maximum(s, 1), n_steps - 1)

    return pl.pallas_call(
        functools.partial(_rg_kernel, nx=nx),
        grid=(b, 3, n_steps + 1),
        in_specs=[pl.BlockSpec((None, n_sub * tt, d), lambda i, p, s: (i, x_tile(p, s), 0)),
                  pl.BlockSpec((None, tt, d), lambda i, p, s: (i, 0, 0)),
                  pl.BlockSpec((None, N_MOD, d), lambda i, p, s: (i, 0, 0)),
                  pl.BlockSpec((None, N_MOD, d), lambda i, p, s: (0, 0, 0)),
                  pl.BlockSpec(g.shape, lambda i, p, s: (0, 0)),
                  _resident((tt, tt)),
                  _resident((tt, tt)),
                  _resident((d, 2 * d)),
                  pl.BlockSpec(conv_w.shape, lambda i, p, s: (0, 0)),
                  pl.BlockSpec((1, d), lambda i, p, s: (0, 0)),
                  _resident(gw.shape),
                  pl.BlockSpec(gb.shape, lambda i, p, s: (0, 0)),
                  pl.BlockSpec(lam.shape, lambda i, p, s: (0, 0)),
                  _resident((d, d))],
        out_specs=[pl.BlockSpec((None, n_sub * tt, d), lambda i, p, s: (i, xo_tile(p, s), 0)),
                   pl.BlockSpec((None, tt, d), lambda i, p, s: (i, 0, 0))],
        out_shape=[jax.ShapeDtypeStruct(x.shape, f32), jax.ShapeDtypeStruct(xc.shape, f32)],
        scratch_shapes=[pltpu.VMEM((rows + 3 * SUBLANES, d), f32),
                        pltpu.VMEM((rows, d), f32),
                        pltpu.VMEM((rows, d), bf16),
                        pltpu.VMEM((rows, d), bf16),
                        pltpu.VMEM((SUBLANES, d), f32)],
        compiler_params=_cparams(3),
        name="rglru",
    )(x, xc, mx, mc, g, perm, perm.T, w_in.astype(bf16), conv_w, conv_b.reshape(1, d), gw, gb, lam,
      w_out.astype(bf16))


def _rope_tables(l):
    rows = l // GRID_W
    row = jnp.repeat(jnp.arange(rows, dtype=f32), GRID_W)
    col = jnp.tile(jnp.arange(GRID_W, dtype=f32), rows)
    inv = 1.0 / (ROPE_BASE ** (jnp.arange(0, ROPE_AXIS_DIM, 2, dtype=f32) / ROPE_AXIS_DIM))
    ar, ac = row[:, None] * inv, col[:, None] * inv
    cos = jnp.concatenate([jnp.cos(ar), jnp.cos(ar), jnp.cos(ac), jnp.cos(ac)], axis=-1)
    sin = jnp.concatenate([-jnp.sin(ar), jnp.sin(ar), -jnp.sin(ac), jnp.sin(ac)], axis=-1)
    reps = LANES // HEAD_DIM
    return jnp.tile(cos, (1, reps)), jnp.tile(sin, (1, reps)), cos.T, sin.T


def _qkv_kernel(x_ref, mod_ref, g_ref, wqt_ref, wk_ref, wvt_ref, cos_ref, sin_ref, cost_ref, sint_ref,
                qt_ref, k_ref, vt_ref):
    tt = x_ref.shape[0]
    dq = wqt_ref.shape[0]
    dkv = wk_ref.shape[1]
    hb = _modulate(x_ref[...], g_ref[1:2], mod_ref[3:4], mod_ref[4:5]).astype(bf16)
    nt = (((1,), (1,)), ((), ()))
    half = ROPE_AXIS_DIM // 2

    qt = lax.dot_general(wqt_ref[...], hb, nt, preferred_element_type=f32)
    cost = cost_ref[...] * Q_SCALE
    sint = sint_ref[...] * Q_SCALE
    for h in range(dq // HEAD_DIM):
        z = qt[h * HEAD_DIM:(h + 1) * HEAD_DIM]
        partner = jnp.concatenate([z[half:2 * half], z[:half], z[3 * half:], z[2 * half:3 * half]], axis=0)
        qt_ref[h * HEAD_DIM:(h + 1) * HEAD_DIM, :] = (z * cost + partner * sint).astype(bf16)

    k = jnp.dot(hb, wk_ref[...], preferred_element_type=f32)
    cos = cos_ref[...]
    sin = sin_ref[...]
    lane = lax.broadcasted_iota(jnp.int32, cos.shape, 1)
    first = (lane % ROPE_AXIS_DIM) < half
    for j in range(dkv // LANES):
        z = k[:, j * LANES:(j + 1) * LANES]
        partner = jnp.where(first, pltpu.roll(z, LANES - half, axis=1), pltpu.roll(z, half, axis=1))
        k_ref[:, j * LANES:(j + 1) * LANES] = (z * cos + partner * sin).astype(bf16)

    vt = lax.dot_general(wvt_ref[...], hb, nt, preferred_element_type=f32).astype(bf16)
    for j in range(tt // LANES):
        vt_ref[j] = vt[:, j * LANES:(j + 1) * LANES]


def _kv_kernel(x_ref, mod_ref, g_ref, w_ref, k_ref, vt_ref, *, dkv):
    h = _modulate(x_ref[...], g_ref[1:2], mod_ref[3:4], mod_ref[4:5])
    kv = jnp.dot(h.astype(bf16), w_ref[...], preferred_element_type=f32)
    k_ref[...] = kv[:, :dkv].astype(bf16)
    vt_ref[...] = kv[:, dkv:].T.astype(bf16)


def _attn_kernel(sink_ref, x_ref, qt_ref, k_ref, vt_ref, kc_ref, vct_ref, mod_ref, g_ref, wo_ref, o_ref):
    for sub in range(x_ref.shape[0] // Q_ROWS):
        _attn_tile(sub, sink_ref, x_ref, qt_ref, k_ref, vt_ref, kc_ref[...], vct_ref[...], mod_ref, g_ref, wo_ref,
                   o_ref)


def _attn_tile(sub, sink_ref, x_ref, qt_ref, k_ref, vt_ref, kc, vct, mod_ref, g_ref, wo_ref, o_ref):
    tq = Q_ROWS
    d = x_ref.shape[1]
    l = k_ref.shape[0]
    span = tq + 2 * WINDOW
    group = d // HEAD_DIM // N_KV_HEADS
    lanes = group * tq
    rows = slice(sub * tq, (sub + 1) * tq)
    start = pl.program_id(1) * x_ref.shape[0] + sub * tq
    ws = pl.multiple_of(jnp.clip(start - WINDOW, 0, l - span), LANES)
    kw = k_ref[pl.ds(ws, span), :]
    slab = ws // LANES
    vtw = jnp.concatenate([vt_ref[slab + i] for i in range(span // LANES)], axis=1)
    qt = qt_ref[:, rows]
    kj = lax.broadcasted_iota(jnp.int32, (span, lanes), 0)
    qi = lax.broadcasted_iota(jnp.int32, (span, lanes), 1) % tq
    valid = jnp.abs(kj - qi + (ws - start)) <= WINDOW

    def with_ones(v):
        row = lax.broadcasted_iota(jnp.int32, (2 * SUBLANES, v.shape[1]), 0)
        return jnp.concatenate([v, (row == 0).astype(bf16)], axis=0)

    def scores(kh):
        heads = range(kh * group, (kh + 1) * group)
        qg = jnp.concatenate([qt[h * HEAD_DIM:(h + 1) * HEAD_DIM] for h in heads], axis=1)
        pair, odd = divmod(kh, LANES // HEAD_DIM)
        zero = jnp.zeros_like(qg)
        qz = jnp.concatenate([zero, qg] if odd else [qg, zero], axis=0)
        kcols = slice(pair * LANES, (pair + 1) * LANES)
        return (jnp.dot(kw[:, kcols], qz, preferred_element_type=f32),
                jnp.dot(kc[:, kcols], qz, preferred_element_type=f32))

    def weights(kh, s_lat, s_ctx):
        heads = range(kh * group, (kh + 1) * group)
        s_lat = jnp.where(valid, s_lat, NEG_INF)
        sink = jnp.concatenate([jnp.full((1, tq), sink_ref[h] * LOG2E, f32) for h in heads], axis=1)
        m = jnp.maximum(jnp.maximum(jnp.max(s_lat, axis=0, keepdims=True),
                                    jnp.max(s_ctx, axis=0, keepdims=True)), sink)
        return jnp.exp2(s_lat - m).astype(bf16), jnp.exp2(s_ctx - m).astype(bf16), jnp.exp2(sink - m)

    def mix(kh, p_lat, p_ctx, p_sink):
        vrows = slice(kh * HEAD_DIM, (kh + 1) * HEAD_DIM)
        o = (jnp.dot(with_ones(vtw[vrows]), p_lat, preferred_element_type=f32)
             + jnp.dot(with_ones(vct[vrows]), p_ctx, preferred_element_type=f32))
        o = o[:HEAD_DIM] / (o[HEAD_DIM:HEAD_DIM + 1] + p_sink)
        return [o[:, i * tq:(i + 1) * tq] for i in range(group)]

    pieces = []
    s_next = scores(0)
    p_prev = None
    for kh in range(N_KV_HEADS):
        s_cur = s_next
        if kh + 1 < N_KV_HEADS:
            s_next = scores(kh + 1)
        if p_prev is not None:
            pieces += mix(kh - 1, *p_prev)
        p_prev = weights(kh, *s_cur)
    pieces += mix(N_KV_HEADS - 1, *p_prev)
    att = jnp.concatenate(pieces, axis=0).T.astype(bf16)
    y = jnp.dot(att, wo_ref[...], preferred_element_type=f32)
    o_ref[rows, :] = x_ref[rows, :] + _rms(y, mod_ref[5:6] * g_ref[4:5])


def _attention(x, xc, mx, mc, g, w_qkv, w_o, sink):
    b, l, d = x.shape
    c = xc.shape[1]
    dq = w_o.shape[0]
    dkv = (w_qkv.shape[1] - dq) // 2
    tt = min(QKV_ROWS, l)
    assert l % tt == 0
    cos, sin, cost, sint = _rope_tables(l)
    wb = w_qkv.astype(bf16)
    wqt, wk, wvt = wb[:, :dq].T, wb[:, dq:dq + dkv], wb[:, dq + dkv:].T
    qt, k, vt = pl.pallas_call(
        _qkv_kernel,
        grid=(b, l // tt),
        in_specs=[pl.BlockSpec((None, tt, d), lambda i, t: (i, t, 0)),
                  pl.BlockSpec((None, N_MOD, d), lambda i, t: (i, 0, 0)),
                  pl.BlockSpec(g.shape, lambda i, t: (0, 0)),
                  _resident(wqt.shape),
                  _resident(wk.shape),
                  _resident(wvt.shape),
                  pl.BlockSpec((tt, LANES), lambda i, t: (t, 0)),
                  pl.BlockSpec((tt, LANES), lambda i, t: (t, 0)),
                  pl.BlockSpec((HEAD_DIM, tt), lambda i, t: (0, t)),
                  pl.BlockSpec((HEAD_DIM, tt), lambda i, t: (0, t))],
        out_specs=[pl.BlockSpec((None, dq, tt), lambda i, t: (i, 0, t)),
                   pl.BlockSpec((None, tt, dkv), lambda i, t: (i, t, 0)),
                   pl.BlockSpec((None, tt // LANES, dkv, LANES), lambda i, t: (i, t, 0, 0))],
        out_shape=[jax.ShapeDtypeStruct((b, dq, l), bf16),
                   jax.ShapeDtypeStruct((b, l, dkv), bf16),
                   jax.ShapeDtypeStruct((b, l // LANES, dkv, LANES), bf16)],
        compiler_params=_cparams(2),
        name="attn_qkv",
    )(x, mx, g, wqt, wk, wvt, cos, sin, cost, sint)
    kc, vct = pl.pallas_call(
        functools.partial(_kv_kernel, dkv=dkv),
        grid=(b,),
        in_specs=[pl.BlockSpec((None, c, d), lambda i: (i, 0, 0)),
                  pl.BlockSpec((None, N_MOD, d), lambda i: (0, 0, 0)),
                  pl.BlockSpec(g.shape, lambda i: (0, 0)),
                  _resident((d, 2 * dkv))],
        out_specs=[pl.BlockSpec((None, c, dkv), lambda i: (i, 0, 0)),
                   pl.BlockSpec((None, dkv, c), lambda i: (i, 0, 0))],
        out_shape=[jax.ShapeDtypeStruct((b, c, dkv), bf16), jax.ShapeDtypeStruct((b, dkv, c), bf16)],
        compiler_params=_cparams(1),
        name="attn_ctx_kv",
    )(xc, mc, g, wb[:, dq:])
    tq = Q_TILES_PER_STEP * Q_ROWS
    assert l % tq == 0 and l >= Q_ROWS + 2 * WINDOW and Q_ROWS == LANES
    return pl.pallas_call(
        _attn_kernel,
        grid=(b, l // tq),
        in_specs=[pl.BlockSpec(memory_space=pltpu.SMEM),
                  pl.BlockSpec((None, tq, d), lambda i, n: (i, n, 0)),
                  pl.BlockSpec((None, dq, tq), lambda i, n: (i, 0, n)),
                  pl.BlockSpec((None, l, dkv), lambda i, n: (i, 0, 0)),
                  pl.BlockSpec((None, l // LANES, dkv, LANES), lambda i, n: (i, 0, 0, 0)),
                  pl.BlockSpec((None, c, dkv), lambda i, n: (i, 0, 0)),
                  pl.BlockSpec((None, dkv, c), lambda i, n: (i, 0, 0)),
                  pl.BlockSpec((None, N_MOD, d), lambda i, n: (i, 0, 0)),
                  pl.BlockSpec(g.shape, lambda i, n: (0, 0)),
                  _resident((dq, d))],
        out_specs=pl.BlockSpec((None, tq, d), lambda i, n: (i, n, 0)),
        out_shape=jax.ShapeDtypeStruct(x.shape, f32),
        compiler_params=_cparams(2),
        name="attn_core",
    )(sink, x, qt, k, vt, kc, vct, mx, g, w_o.astype(bf16))


def kernel(x, c, ctx, c_ctx, w_mod, b_mod, norm_g, ffn_w_in, ffn_w_out, rg_w_in, rg_conv_w, rg_conv_b,
           rg_gate_w, rg_gate_b, rg_lambda, rg_w_out, attn_w_qkv, attn_w_o, attn_sink):
    b, l, d = x.shape
    n_ctx = ctx.shape[1]
    depth = w_mod.shape[0]
    assert depth == 2, "layer 0 = RG-LRU with context output, layer 1 = windowed attention (last)"
    rows = b + 1
    rows_pad = -(-rows // 16) * 16
    cond = jnp.zeros((rows_pad, d), f32).at[:b].set(c).at[b].set(c_ctx)
    mods = _modulation(cond, w_mod, b_mod)
    xc = ctx
    for i in range(depth):
        mx, mc, g = mods[i, :b], mods[i, b:b + 1], norm_g[i]
        flat = lambda a: a.reshape(1, b * n_ctx, d)
        w1 = _ffn_weights(ffn_w_in[i, 0], ffn_w_out[i, 0])
        w2 = _ffn_weights(ffn_w_in[i, 1], ffn_w_out[i, 1])
        x = _ffn(x, mx, g, w1, 0)
        xc = _ffn(flat(xc), mc, g, w1, 0).reshape(b, n_ctx, d)
        if i == 0:
            x, xc = _rglru(x, xc, mx, mc, g, rg_w_in[0], rg_conv_w[0], rg_conv_b[0], rg_gate_w[0],
                           rg_gate_b[0], rg_lambda[0], rg_w_out[0])
            x = _ffn(x, mx, g, w2, 2)
            xc = _ffn(flat(xc), mc, g, w2, 2).reshape(b, n_ctx, d)
        else:
            x = _attention(x, xc, mx, mc, g, attn_w_qkv[0], attn_w_o[0], attn_sink[0])
            x = _ffn(x, mx, g, w2, 2)
    return x
```
